```python
import functools
import jax, jax.numpy as jnp
from jax import lax
import numpy as np

D_MODEL = 2048
BATCH = 1
SEQ = 8192
DEPTH = 2
DEC_BATCH = 32
DEC_SEQ = 1
PAST_LEN = 8192
PAGE_SIZE = 128

FOX_HD = 128
FOX_W = D_MODEL // 2
FOX_H = FOX_W // FOX_HD
RW_N = 64
RW_W = D_MODEL - FOX_W
RW_H = RW_W // RW_N
R_DECAY = 64
R_A = 64
R_G = 160
D_FF = ((8 * D_MODEL // 3 + 255) // 256) * 256
Q_BLOCK = 128
C_FOX_IN = 3 * FOX_W + FOX_H
C_SHIFT = 3 * RW_W + R_DECAY + R_A + R_G
C_IN = C_FOX_IN + C_SHIFT
RMS_EPS = 1e-6
LNX_EPS = 64e-5
FFN_RES = 0.5

kernel_name = "fox_rwkv7_macaron_hybrid_step"


def rmsnorm(x, g):
    xf = x.astype(jnp.float32)
    xf = xf * lax.rsqrt(jnp.mean(xf * xf, axis=-1, keepdims=True) + RMS_EPS)
    return (xf * g.astype(jnp.float32)).astype(x.dtype)


def swiglu(x, w1, w3, w2):
    return (jax.nn.silu(x @ w1) * (x @ w3)) @ w2


def fox_attend(q, k, v, cq, ck, q_pos, k_pos):
    s = jnp.einsum("bqhd,bkhd->bhqk", q.astype(jnp.float32), k.astype(jnp.float32)) * (FOX_HD ** -0.5)
    s = s + jnp.swapaxes(cq, 1, 2)[..., :, None] - jnp.swapaxes(ck, 1, 2)[..., None, :]
    causal = k_pos[None, :] <= q_pos[:, None]
    s = jnp.where(causal[None, None], s, -jnp.inf)
    p = jax.nn.softmax(s, axis=-1)
    return jnp.einsum("bhqk,bkhd->bqhd", p, v.astype(jnp.float32)).astype(q.dtype)


def fox_prompt(q, k, v, logf):
    t = q.shape[1]
    c = jnp.cumsum(logf.astype(jnp.float32), axis=1)
    k_pos = jnp.arange(t)

    def block(i):
        start = i * Q_BLOCK
        qb = lax.dynamic_slice_in_dim(q, start, Q_BLOCK, axis=1)
        cqb = lax.dynamic_slice_in_dim(c, start, Q_BLOCK, axis=1)
        return fox_attend(qb, k, v, cqb, c, start + jnp.arange(Q_BLOCK), k_pos)

    o = lax.map(block, jnp.arange(t // Q_BLOCK))
    return jnp.moveaxis(o, 0, 1).reshape(q.shape)


def fox_sample(k_past, v_past, logf_past, q, k, v, logf):
    past = k_past.shape[1]
    t = q.shape[1]
    k_all = jnp.concatenate([k_past.astype(k.dtype), k], axis=1)
    v_all = jnp.concatenate([v_past.astype(v.dtype), v], axis=1)
    c = jnp.cumsum(jnp.concatenate([logf_past.astype(jnp.float32), logf.astype(jnp.float32)], axis=1), axis=1)
    k_pos = jnp.arange(past + t)
    q_pos = past + jnp.arange(t)
    return fox_attend(q, k_all, v_all, c[:, past:], c, q_pos, k_pos)


def wkv7(s0, r, log_decay, k, v, kk, a):
    def step(s, inp):
        r_t, ld_t, k_t, v_t, kk_t, a_t = inp
        s_kk = jnp.einsum("bhvk,bhk->bhv", s, kk_t)
        s = (s * jnp.exp(ld_t)[:, :, None, :]
             - s_kk[..., None] * (kk_t * a_t)[:, :, None, :]
             + v_t[..., None] * k_t[:, :, None, :])
        return s, jnp.einsum("bhvk,bhk->bhv", s, r_t)

    seq = tuple(jnp.moveaxis(t_.astype(jnp.float32), 1, 0) for t_ in (r, log_decay, k, v, kk, a))
    s, ys = lax.scan(step, s0.astype(jnp.float32), seq)
    return jnp.moveaxis(ys, 0, 1), s


def hybrid_mix(h, lp, attend, rwkv_s0, shift_prev):
    b, t, _ = h.shape
    proj = h @ lp["w_in"]
    q = proj[..., :FOX_W].reshape(b, t, FOX_H, FOX_HD)
    k = proj[..., FOX_W:2 * FOX_W].reshape(b, t, FOX_H, FOX_HD)
    v = proj[..., 2 * FOX_W:3 * FOX_W].reshape(b, t, FOX_H, FOX_HD)
    logf = jax.nn.log_sigmoid((proj[..., 3 * FOX_W:C_FOX_IN] + lp["b_f"]).astype(jnp.float32))
    o_fox = attend(q, k, v, logf)
    o_fox = rmsnorm(o_fox, lp["fox_gain"].reshape(FOX_H, FOX_HD)).reshape(b, t, FOX_W)
    pr = proj[..., C_FOX_IN:]
    prev = jnp.concatenate([shift_prev[:, None, :].astype(pr.dtype), pr[:, :-1]], axis=1)
    xs = pr + (prev - pr) * lp["mu"]
    r = xs[..., :RW_W]
    kr = xs[..., RW_W:2 * RW_W]
    vr = xs[..., 2 * RW_W:3 * RW_W]
    off = 3 * RW_W
    wd = xs[..., off:off + R_DECAY]
    ad = xs[..., off + R_DECAY:off + R_DECAY + R_A]
    gd = xs[..., off + R_DECAY + R_A:]
    w_log = -jax.nn.softplus(-(lp["w0"] + jnp.tanh(wd) @ lp["w_up"])) - 0.5
    log_decay = -jnp.exp(w_log.astype(jnp.float32))
    a = jax.nn.sigmoid(lp["a0"] + ad @ lp["a_up"])
    g = jax.nn.sigmoid(gd) @ lp["g_up"]

    def heads(z):
        return z.reshape(b, t, RW_H, RW_N)

    kk = heads((kr * lp["k_k"]).astype(jnp.float32))
    kk = kk / jnp.maximum(jnp.sqrt(jnp.sum(kk * kk, axis=-1, keepdims=True)), 1e-12)
    kr = kr * (1.0 + (a - 1.0) * lp["k_a"])
    y, s_new = wkv7(rwkv_s0, heads(r), heads(log_decay), heads(kr), heads(vr), kk, heads(a))
    mean = jnp.mean(y, axis=-1, keepdims=True)
    var = jnp.mean(jnp.square(y - mean), axis=-1, keepdims=True)
    yn = ((y - mean) * lax.rsqrt(var + LNX_EPS)).reshape(b, t, RW_W)
    yn = yn * lp["lnx_w"].astype(jnp.float32) + lp["lnx_b"].astype(jnp.float32)
    bonus = jnp.sum(heads(r).astype(jnp.float32) * heads(kr).astype(jnp.float32) * lp["r_k"].astype(jnp.float32), axis=-1, keepdims=True) * heads(vr).astype(jnp.float32)
    o_rw = ((yn + bonus.reshape(b, t, RW_W)) * g.astype(jnp.float32)).astype(h.dtype)
    out = jnp.concatenate([o_fox, o_rw], axis=-1) @ lp["w_out"]
    return out, (k, v, logf), s_new, pr[:, -1]


def layer(x, lp, attend, rwkv_s0, shift_prev):
    x = x + FFN_RES * swiglu(rmsnorm(x, lp["norm_ffa"]), lp["ffa_w1"], lp["ffa_w3"], lp["ffa_w2"])
    mix, kvf, s_new, shift_new = hybrid_mix(rmsnorm(x, lp["norm_mix"]), lp, attend, rwkv_s0, shift_prev)
    x = x + mix
    x = x + FFN_RES * swiglu(rmsnorm(x, lp["norm_ffb"]), lp["ffb_w1"], lp["ffb_w3"], lp["ffb_w2"])
    return x, kvf, s_new, shift_new


def setup_inputs(seed: int = 0) -> dict:
    key = jax.random.key(seed)
    ks = jax.random.split(key, 40)
    n_pages = PAST_LEN // PAGE_SIZE
    n_used = DEC_BATCH * n_pages
    n_pool = n_used + (n_used + 3) // 4
    f32 = jnp.float32

    def nrm(k, shape, scale):
        return jax.random.normal(k, shape, f32) * scale

    def gain(k, shape):
        return 1.0 + nrm(k, shape, 0.02)

    page_table = jax.random.permutation(ks[0], n_pool)[:n_used].reshape(DEC_BATCH, n_pages).astype(jnp.int32)
    return {
        "x_prompt": nrm(ks[1], (BATCH, SEQ, D_MODEL), 1.0),
        "x_sample": nrm(ks[2], (DEC_BATCH, DEC_SEQ, D_MODEL), 1.0),
        "cache_k": nrm(ks[3], (DEPTH, n_pool, PAGE_SIZE, FOX_H, FOX_HD), 1.0),
        "cache_v": nrm(ks[4], (DEPTH, n_pool, PAGE_SIZE, FOX_H, FOX_HD), 1.0),
        "cache_logf": jax.nn.log_sigmoid(3.0 + jax.random.normal(ks[5], (DEPTH, n_pool, PAGE_SIZE, FOX_H), f32)),
        "state_rwkv": nrm(ks[6], (DEPTH, DEC_BATCH, RW_H, RW_N, RW_N), 0.3),
        "state_shift": nrm(ks[7], (DEPTH, DEC_BATCH, C_SHIFT), 1.0),
        "page_table": page_table,
        "norm_ffa": gain(ks[8], (DEPTH, D_MODEL)),
        "ffa_w1": nrm(ks[9], (DEPTH, D_MODEL, D_FF), D_MODEL ** -0.5),
        "ffa_w3": nrm(ks[10], (DEPTH, D_MODEL, D_FF), D_MODEL ** -0.5),
        "ffa_w2": nrm(ks[11], (DEPTH, D_FF, D_MODEL), D_FF ** -0.5),
        "norm_mix": gain(ks[12], (DEPTH, D_MODEL)),
        "w_in": nrm(ks[13], (DEPTH, D_MODEL, C_IN), D_MODEL ** -0.5),
        "b_f": jax.random.uniform(ks[14], (DEPTH, FOX_H), f32, 2.0, 5.0),
        "fox_gain": gain(ks[15], (DEPTH, FOX_W)),
        "mu_shift": jax.random.uniform(ks[16], (DEPTH, C_SHIFT), f32, 0.0, 1.0),
        "w0": jax.random.uniform(ks[17], (DEPTH, RW_W), f32, -5.0, -0.5),
        "w_up": nrm(ks[18], (DEPTH, R_DECAY, RW_W), 0.5 * R_DECAY ** -0.5),
        "a0": nrm(ks[19], (DEPTH, RW_W), 0.1),
        "a_up": nrm(ks[20], (DEPTH, R_A, RW_W), 0.5 * R_A ** -0.5),
        "g_up": nrm(ks[21], (DEPTH, R_G, RW_W), R_G ** -0.5),
        "k_k": 0.85 + nrm(ks[22], (DEPTH, RW_W), 0.02),
        "k_a": 1.0 + nrm(ks[23], (DEPTH, RW_W), 0.02),
        "r_k": nrm(ks[24], (DEPTH, RW_H, RW_N), 0.1),
        "lnx_w": gain(ks[25], (DEPTH, RW_W)),
        "lnx_b": nrm(ks[26], (DEPTH, RW_W), 0.02),
        "w_out": nrm(ks[27], (DEPTH, D_MODEL, D_MODEL), D_MODEL ** -0.5),
        "norm_ffb": gain(ks[28], (DEPTH, D_MODEL)),
        "ffb_w1": nrm(ks[29], (DEPTH, D_MODEL, D_FF), D_MODEL ** -0.5),
        "ffb_w3": nrm(ks[30], (DEPTH, D_MODEL, D_FF), D_MODEL ** -0.5),
        "ffb_w2": nrm(ks[31], (DEPTH, D_FF, D_MODEL), D_FF ** -0.5),
        "norm_final": gain(ks[32], (D_MODEL,)),
    }


def reference(x_prompt, x_sample, cache_k, cache_v, cache_logf, state_rwkv, state_shift, page_table,
              norm_ffa, ffa_w1, ffa_w3, ffa_w2, norm_mix, w_in, b_f, fox_gain, mu_shift, w0, w_up,
              a0, a_up, g_up, k_k, k_a, r_k, lnx_w, lnx_b, w_out, norm_ffb, ffb_w1, ffb_w3, ffb_w2,
              norm_final):
    n_pages = PAST_LEN // PAGE_SIZE
    b_p = x_prompt.shape[0]
    b_s = x_sample.shape[0]
    xp = x_prompt
    xs = x_sample
    kp_l, vp_l, fp_l, sp_l, hp_l = [], [], [], [], []
    ks_l, vs_l, fs_l, ss_l, hs_l = [], [], [], [], []
    for l in range(DEPTH):
        lp = {
            "norm_ffa": norm_ffa[l], "ffa_w1": ffa_w1[l], "ffa_w3": ffa_w3[l], "ffa_w2": ffa_w2[l],
            "norm_mix": norm_mix[l], "w_in": w_in[l], "b_f": b_f[l], "fox_gain": fox_gain[l],
            "mu": mu_shift[l], "w0": w0[l], "w_up": w_up[l], "a0": a0[l], "a_up": a_up[l],
            "g_up": g_up[l], "k_k": k_k[l], "k_a": k_a[l], "r_k": r_k[l], "lnx_w": lnx_w[l],
            "lnx_b": lnx_b[l], "w_out": w_out[l],
            "norm_ffb": norm_ffb[l], "ffb_w1": ffb_w1[l], "ffb_w3": ffb_w3[l], "ffb_w2": ffb_w2[l],
        }
        xp, (kp, vp, fp), sp, hp = layer(
            xp, lp, fox_prompt,
            jnp.zeros((b_p, RW_H, RW_N, RW_N), jnp.float32),
            jnp.zeros((b_p, C_SHIFT), xp.dtype))
        k_past = cache_k[l, page_table].reshape(b_s, n_pages * PAGE_SIZE, FOX_H, FOX_HD)
        v_past = cache_v[l, page_table].reshape(b_s, n_pages * PAGE_SIZE, FOX_H, FOX_HD)
        f_past = cache_logf[l, page_table].reshape(b_s, n_pages * PAGE_SIZE, FOX_H)
        attend_sample = functools.partial(fox_sample, k_past, v_past, f_past)
        xs, (ks_, vs_, fs_), ss, hs = layer(xs, lp, attend_sample, state_rwkv[l], state_shift[l])
        kp_l.append(kp); vp_l.append(vp); fp_l.append(fp); sp_l.append(sp); hp_l.append(hp)
        ks_l.append(ks_); vs_l.append(vs_); fs_l.append(fs_); ss_l.append(ss); hs_l.append(hs)
    y_prompt = rmsnorm(xp, norm_final)
    y_sample = rmsnorm(xs, norm_final)
    k_prompt = jnp.stack(kp_l)
    v_prompt = jnp.stack(vp_l)
    logf_prompt = jnp.stack(fp_l)
    rwkv_prompt = jnp.stack(sp_l)
    shift_prompt = jnp.stack(hp_l)
    k_sample = jnp.stack(ks_l)
    v_sample = jnp.stack(vs_l)
    logf_sample = jnp.stack(fs_l)
    rwkv_sample = jnp.stack(ss_l)
    shift_sample = jnp.stack(hs_l)
    return (y_prompt, y_sample, k_prompt, v_prompt, logf_prompt, rwkv_prompt, shift_prompt,
            k_sample, v_sample, logf_sample, rwkv_sample, shift_sample)
```

```python
import functools

import jax
import jax.numpy as jnp
from jax import lax
from jax.experimental import pallas as pl
from jax.experimental.pallas import tpu as pltpu

F32 = jnp.float32
BF16 = jnp.bfloat16

D_MODEL = 2048
FOX_HD = 128
FOX_W = D_MODEL // 2
FOX_H = FOX_W // FOX_HD
RW_N = 64
RW_W = D_MODEL - FOX_W
RW_H = RW_W // RW_N
R_DECAY = 64
R_A = 64
R_G = 160
PAGE_SIZE = 128
C_FOX_IN = 3 * FOX_W + FOX_H
C_SHIFT = 3 * RW_W + R_DECAY + R_A + R_G
RMS_EPS = 1e-6
LNX_EPS = 64e-5
FFN_RES = 0.5

LANE = 128
PR_WA = 3 * RW_W
PR_F = PR_WA + LANE
PR_G = PR_F + LANE
PR_W = PR_G + 2 * LANE
WKV_L = 64
VMEM_LIMIT = 56 * 1024 * 1024


def _cparams(sem):
    return pltpu.CompilerParams(dimension_semantics=sem, vmem_limit_bytes=VMEM_LIMIT)


def _dot(a, b):
    return jnp.dot(a, b, preferred_element_type=F32)


def _dot_nt(a, b):
    return lax.dot_general(a, b, (((1,), (1,)), ((), ())), preferred_element_type=F32)


def _dot_tn(a, b):
    return lax.dot_general(a, b, (((0,), (0,)), ((), ())), preferred_element_type=F32)


def _split(x):
    hi = x.astype(BF16)
    lo = (x - hi.astype(F32)).astype(BF16)
    return hi, lo


def _rms(x, g):
    return x * lax.rsqrt(jnp.mean(x * x, axis=-1, keepdims=True) + RMS_EPS) * g


def _ffn_kernel(x_ref, g_ref, w1_ref, w3_ref, w2_ref, o_ref, hn_ref):
    @pl.when(pl.program_id(1) == 0)
    def _():
        x = x_ref[...]
        hn_ref[...] = _rms(x, g_ref[...]).astype(BF16)
        o_ref[...] = x

    hn = hn_ref[...]
    h1 = _dot(hn, w1_ref[...])
    h3 = _dot(hn, w3_ref[...])
    a = (h1 * jax.nn.sigmoid(h1) * h3 * FFN_RES).astype(BF16)
    o_ref[...] += _dot(a, w2_ref[...])


def _ffn(x, g, w1, w3, w2, tm, tf):
    t, d = x.shape
    f = w1.shape[1]
    return pl.pallas_call(
        _ffn_kernel,
        out_shape=jax.ShapeDtypeStruct((t, d), F32),
        grid=(t // tm, f // tf),
        in_specs=[
            pl.BlockSpec((tm, d), lambda i, j: (i, 0)),
            pl.BlockSpec((1, d), lambda i, j: (0, 0)),
            pl.BlockSpec((d, tf), lambda i, j: (0, j)),
            pl.BlockSpec((d, tf), lambda i, j: (0, j)),
            pl.BlockSpec((tf, d), lambda i, j: (j, 0)),
        ],
        out_specs=pl.BlockSpec((tm, d), lambda i, j: (i, 0)),
        scratch_shapes=[pltpu.VMEM((tm, d), BF16)],
        compiler_params=_cparams(("parallel", "arbitrary")),
        name="ffn",
    )(x, g, w1, w3, w2)


def _norm_matmul_kernel(x_ref, g_ref, w_ref, o_ref, hn_ref):
    @pl.when(pl.program_id(1) == 0)
    def _():
        hn_ref[...] = _rms(x_ref[...], g_ref[...]).astype(BF16)

    o_ref[...] = _dot(hn_ref[...], w_ref[...])


def _norm_matmul(x, g, w, tm, tn):
    t, d = x.shape
    n = w.shape[1]
    return pl.pallas_call(
        _norm_matmul_kernel,
        out_shape=jax.ShapeDtypeStruct((t, n), F32),
        grid=(t // tm, n // tn),
        in_specs=[
            pl.BlockSpec((tm, d), lambda i, j: (i, 0)),
            pl.BlockSpec((1, d), lambda i, j: (0, 0)),
            pl.BlockSpec((d, tn), lambda i, j: (0, j)),
        ],
        out_specs=pl.BlockSpec((tm, tn), lambda i, j: (i, j)),
        scratch_shapes=[pltpu.VMEM((tm, d), BF16)],
        compiler_params=_cparams(("parallel", "arbitrary")),
        name="norm_matmul",
    )(x, g, w)


def _log_sigmoid(z):
    return jnp.minimum(z, 0.0) - jnp.log1p(jnp.exp(-jnp.abs(z)))


def _logf_kernel(f_ref, bf_ref, tri_ref, lf_ref, c_ref, ct_ref, carry_ref):
    @pl.when(pl.program_id(0) == 0)
    def _():
        carry_ref[...] = jnp.zeros_like(carry_ref)

    lf = _log_sigmoid(f_ref[...] + bf_ref[...])
    hi, lo = _split(lf)
    c = _dot(tri_ref[...], hi) + _dot(tri_ref[...], lo) + carry_ref[...]
    carry_ref[...] = c[-1:, :]
    lf_ref[...] = lf[:, :FOX_H]
    c_ref[...] = c[:, :FOX_H]
    ct_ref[...] = c.T[:FOX_H, :]


def _logf(pr, bf_row, tm):
    t = pr.shape[0]
    tri = (lax.broadcasted_iota(jnp.int32, (tm, tm), 0) >= lax.broadcasted_iota(jnp.int32, (tm, tm), 1)).astype(BF16)
    return pl.pallas_call(
        _logf_kernel,
        out_shape=(
            jax.ShapeDtypeStruct((t, FOX_H), F32),
            jax.ShapeDtypeStruct((t, FOX_H), F32),
            jax.ShapeDtypeStruct((FOX_H, t), F32),
        ),
        grid=(t // tm,),
        in_specs=[
            pl.BlockSpec((tm, LANE), lambda i: (i, PR_F // LANE)),
            pl.BlockSpec((1, LANE), lambda i: (0, 0)),
            pl.BlockSpec((tm, tm), lambda i: (0, 0)),
        ],
        out_specs=(
            pl.BlockSpec((tm, FOX_H), lambda i: (i, 0)),
            pl.BlockSpec((tm, FOX_H), lambda i: (i, 0)),
            pl.BlockSpec((FOX_H, tm), lambda i: (0, i)),
        ),
        scratch_shapes=[pltpu.VMEM((1, LANE), F32)],
        compiler_params=_cparams(("arbitrary",)),
        name="logf_cumsum",
    )(pr, bf_row, tri)


def _logf_only_kernel(f_ref, bf_ref, lf_ref):
    lf_ref[...] = _log_sigmoid(f_ref[...] + bf_ref[...])[:, :FOX_H]


def _logf_only(pr, bf_row):
    t = pr.shape[0]
    return pl.pallas_call(
        _logf_only_kernel,
        out_shape=jax.ShapeDtypeStruct((t, FOX_H), F32),
        grid=(1,),
        in_specs=[
            pl.BlockSpec((t, LANE), lambda i: (0, PR_F // LANE)),
            pl.BlockSpec((1, LANE), lambda i: (0, 0)),
        ],
        out_specs=pl.BlockSpec((t, FOX_H), lambda i: (0, 0)),
        name="logf",
    )(pr, bf_row)


def _fox_prompt_kernel(q_ref, k_ref, v_ref, cq_ref, ck_ref, gain_ref, o_ref, m_ref, l_ref, acc_ref, *, tq):
    qi = pl.program_id(0)
    ki = pl.program_id(1)

    @pl.when(ki == 0)
    def _():
        m_ref[...] = jnp.full_like(m_ref, -jnp.inf)
        l_ref[...] = jnp.zeros_like(l_ref)
        acc_ref[...] = jnp.zeros_like(acc_ref)

    def step(diagonal):
        cq = cq_ref[...]
        ck = ck_ref[...]
        if diagonal:
            keep = (lax.broadcasted_iota(jnp.int32, (tq, tq), 0) >= lax.broadcasted_iota(jnp.int32, (tq, tq), 1))
        for h in range(FOX_H):
            sl = slice(h * FOX_HD, (h + 1) * FOX_HD)
            q = (q_ref[:, sl] * (FOX_HD ** -0.5)).astype(BF16)
            s = _dot_nt(q, k_ref[:, sl].astype(BF16))
            s = s + (cq[:, h:h + 1] - ck[h:h + 1, :])
            if diagonal:
                s = jnp.where(keep, s, -jnp.inf)
            m_prev = m_ref[h]
            m_new = jnp.maximum(m_prev, jnp.max(s, axis=-1, keepdims=True))
            p = jnp.exp(s - m_new)
            alpha = jnp.exp(m_prev - m_new)
            l_ref[h] = alpha * l_ref[h] + jnp.sum(p, axis=-1, keepdims=True)
            acc_ref[:, sl] = alpha * acc_ref[:, sl] + _dot(p.astype(BF16), v_ref[:, sl].astype(BF16))
            m_ref[h] = m_new

    @pl.when(ki < qi)
    def _():
        step(False)

    @pl.when(ki == qi)
    def _():
        step(True)
        for h in range(FOX_H):
            sl = slice(h * FOX_HD, (h + 1) * FOX_HD)
            o = acc_ref[:, sl] / l_ref[h]
            o_ref[:, sl] = _rms(o, gain_ref[:, sl])


def _fox_prompt(qkv, c, ct, gain_row, tq):
    t = qkv.shape[0]
    n = t // tq
    return pl.pallas_call(
        functools.partial(_fox_prompt_kernel, tq=tq),
        out_shape=jax.ShapeDtypeStruct((t, FOX_W), F32),
        grid=(n, n),
        in_specs=[
            pl.BlockSpec((tq, FOX_W), lambda i, j: (i, 0)),
            pl.BlockSpec((tq, FOX_W), lambda i, j: (jnp.minimum(i, j), 1)),
            pl.BlockSpec((tq, FOX_W), lambda i, j: (jnp.minimum(i, j), 2)),
            pl.BlockSpec((tq, FOX_H), lambda i, j: (i, 0)),
            pl.BlockSpec((FOX_H, tq), lambda i, j: (0, jnp.minimum(i, j))),
            pl.BlockSpec((1, FOX_W), lambda i, j: (0, 0)),
        ],
        out_specs=pl.BlockSpec((tq, FOX_W), lambda i, j: (i, 0)),
        scratch_shapes=[
            pltpu.VMEM((FOX_H, tq, 1), F32),
            pltpu.VMEM((FOX_H, tq, 1), F32),
            pltpu.VMEM((tq, FOX_W), F32),
        ],
        compiler_params=_cparams(("parallel", "arbitrary")),
        name="fox_prompt",
    )(qkv, qkv, qkv, c, ct, gain_row)


def _fox_sample_kernel(pt_ref, q_ref, kn_ref, vn_ref, fn_ref, kc_ref, vc_ref, fc_ref, u2_ref, gain_ref, o_ref,
                       m_ref, l_ref, acc_ref, carry_ref):
    del pt_ref
    p_idx = pl.program_id(1)
    n_flat = PAGE_SIZE * FOX_H
    q = q_ref[0] * (FOX_HD ** -0.5)

    @pl.when(p_idx == 0)
    def _():
        m_ref[...] = jnp.sum(q * kn_ref[0], axis=-1, keepdims=True)
        l_ref[...] = jnp.ones_like(l_ref)
        acc_ref[...] = vn_ref[0]
        carry_ref[...] = fn_ref[0]

    kf = kc_ref[0].reshape(n_flat, FOX_HD).astype(BF16)
    vf = vc_ref[0].reshape(n_flat, FOX_HD).astype(BF16)
    s = _dot_nt(q.astype(BF16), kf)
    f_page = fc_ref[0]
    f_hi, f_lo = _split(f_page)
    carry = carry_ref[...]
    bias = carry + _dot(f_hi, u2_ref[...]) + _dot(f_lo, u2_ref[...])
    own = ((lax.broadcasted_iota(jnp.int32, (FOX_H, n_flat), 1) % FOX_H)
           == lax.broadcasted_iota(jnp.int32, (FOX_H, n_flat), 0))
    s = jnp.where(own, s + bias, -jnp.inf)
    carry_ref[...] = carry + jnp.sum(f_page, axis=-1, keepdims=True)

    m_prev = m_ref[...]
    m_new = jnp.maximum(m_prev, jnp.max(s, axis=-1, keepdims=True))
    p = jnp.exp(s - m_new)
    alpha = jnp.exp(m_prev - m_new)
    l_ref[...] = alpha * l_ref[...] + jnp.sum(p, axis=-1, keepdims=True)
    acc_ref[...] = alpha * acc_ref[...] + _dot(p.astype(BF16), vf)
    m_ref[...] = m_new

    @pl.when(p_idx == pl.num_programs(1) - 1)
    def _():
        o_ref[0] = _rms(acc_ref[...] / l_ref[...], gain_ref[...])


def _fox_sample(page_table, q, k_new, v_new, f_new, cache_k, cache_v, cache_ft, gain, layer):
    b = q.shape[0]
    n_pages = page_table.shape[1]
    last = n_pages - 1
    n_flat = PAGE_SIZE * FOX_H
    u2 = (jnp.arange(PAGE_SIZE)[:, None] > (jnp.arange(n_flat) // FOX_H)[None, :]).astype(BF16)
    tok = pl.BlockSpec((1, FOX_H, FOX_HD), lambda i, p, pt: (i, 0, 0))
    page_kv = pl.BlockSpec((None, 1, PAGE_SIZE, FOX_H, FOX_HD), lambda i, p, pt: (layer, pt[i, last - p], 0, 0, 0))
    page_f = pl.BlockSpec((None, 1, FOX_H, PAGE_SIZE), lambda i, p, pt: (layer, pt[i, last - p], 0, 0))
    grid_spec = pltpu.PrefetchScalarGridSpec(
        num_scalar_prefetch=1,
        grid=(b, n_pages),
        in_specs=[
            tok, tok, tok,
            pl.BlockSpec((1, FOX_H, 1), lambda i, p, pt: (i, 0, 0)),
            page_kv, page_kv, page_f,
            pl.BlockSpec((PAGE_SIZE, n_flat), lambda i, p, pt: (0, 0)),
            pl.BlockSpec((FOX_H, FOX_HD), lambda i, p, pt: (0, 0)),
        ],
        out_specs=tok,
        scratch_shapes=[
            pltpu.VMEM((FOX_H, 1), F32),
            pltpu.VMEM((FOX_H, 1), F32),
            pltpu.VMEM((FOX_H, FOX_HD), F32),
            pltpu.VMEM((FOX_H, 1), F32),
        ],
    )
    return pl.pallas_call(
        _fox_sample_kernel,
        out_shape=jax.ShapeDtypeStruct((b, FOX_H, FOX_HD), F32),
        grid_spec=grid_spec,
        compiler_params=_cparams(("parallel", "arbitrary")),
        name="fox_sample",
    )(page_table, q, k_new, v_new, f_new, cache_k, cache_v, cache_ft, u2, gain)


def _head_sum(x, ind, ind_t):
    hi, lo = _split(x)
    s = _dot(hi, ind) + _dot(lo, ind)
    s_hi, s_lo = _split(s)
    return _dot(s_hi, ind_t) + _dot(s_lo, ind_t)


def _rwkv_prep_kernel(pr_ref, prev_ref, mu_ref, vec_ref, wwa_ref, wg_ref, ind_ref, indt_ref,
                      r_ref, lw_ref, k_ref, v_ref, kk_ref, b_ref, g_ref, bonus_ref, carry_ref, *, sequential):
    pr = pr_ref[...]
    if sequential:
        @pl.when(pl.program_id(0) == 0)
        def _():
            carry_ref[...] = jnp.zeros_like(carry_ref)

        rolled = pltpu.roll(pr, 1, 0)
        first = lax.broadcasted_iota(jnp.int32, pr.shape, 0) == 0
        prev = jnp.where(first, carry_ref[...], rolled)
        carry_ref[...] = pr[-1:, :]
    else:
        prev = prev_ref[...]
    xs = pr + (prev - pr) * mu_ref[...]
    r = xs[:, 0:RW_W]
    kr = xs[:, RW_W:2 * RW_W]
    vr = xs[:, 2 * RW_W:3 * RW_W]
    wa = xs[:, PR_WA:PR_WA + LANE]
    gd = xs[:, PR_G:PR_W]
    w0, a0, k_k, k_a, r_k = (vec_ref[i:i + 1, :] for i in range(5))

    is_decay = lax.broadcasted_iota(jnp.int32, wa.shape, 1) < R_DECAY
    wa_act = jnp.where(is_decay, jnp.tanh(wa), wa).astype(BF16)
    lora = _dot(wa_act, wwa_ref[...])
    x_w = w0 + lora[:, :RW_W]
    w_log = -(jnp.maximum(-x_w, 0.0) + jnp.log1p(jnp.exp(-jnp.abs(x_w)))) - 0.5
    lw_ref[...] = -jnp.exp(w_log)
    a = jax.nn.sigmoid(a0 + lora[:, RW_W:])
    g_ref[...] = _dot(jax.nn.sigmoid(gd).astype(BF16), wg_ref[...])

    kk = kr * k_k
    ss = _head_sum(kk * kk, ind_ref[...], indt_ref[...])
    kk = kk / jnp.maximum(jnp.sqrt(ss), 1e-12)
    k2 = kr * (1.0 + (a - 1.0) * k_a)
    bonus_ref[...] = _head_sum(r * k2 * r_k, ind_ref[...], indt_ref[...]) * vr
    r_ref[...] = r
    k_ref[...] = k2
    v_ref[...] = vr
    kk_ref[...] = kk
    b_ref[...] = kk * a


def _rwkv_prep(pr, prev, mu_row, vecs, w_wa, w_g, ind, ind_t, tm, sequential):
    t = pr.shape[0]
    row = lambda i: (i, 0)
    const = lambda i: (0, 0)
    out = jax.ShapeDtypeStruct((t, RW_W), F32)
    return pl.pallas_call(
        functools.partial(_rwkv_prep_kernel, sequential=sequential),
        out_shape=(out,) * 8,
        grid=(t // tm,),
        in_specs=[
            pl.BlockSpec((tm, PR_W), row),
            pl.BlockSpec((tm, PR_W), row),
            pl.BlockSpec((1, PR_W), const),
            pl.BlockSpec((8, RW_W), const),
            pl.BlockSpec((LANE, 2 * RW_W), const),
            pl.BlockSpec((2 * LANE, RW_W), const),
            pl.BlockSpec((RW_W, LANE), const),
            pl.BlockSpec((LANE, RW_W), const),
        ],
        out_specs=(pl.BlockSpec((tm, RW_W), row),) * 8,
        scratch_shapes=[pltpu.VMEM((1, PR_W), F32)],
        compiler_params=_cparams(("arbitrary",)),
        name="rwkv_prep",
    )(pr, prev, mu_row, vecs, w_wa, w_g, ind, ind_t)


def _wkv_chunk_kernel(r_ref, lw_ref, k_ref, v_ref, kk_ref, b_ref, s0_ref, y_ref, s_ref):
    L = WKV_L

    @pl.when(pl.program_id(0) == 0)
    def _():
        s_ref[...] = s0_ref[...]

    lw = lw_ref[...]
    tril = (lax.broadcasted_iota(jnp.int32, (L, L), 0) >= lax.broadcasted_iota(jnp.int32, (L, L), 1)).astype(BF16)
    lw_hi, lw_lo = _split(lw)
    gcum = _dot(tril, lw_hi) + _dot(tril, lw_lo)
    e_g = jnp.exp(gcum)
    e_gi = jnp.exp(-gcum)
    rt = r_ref[...] * e_g
    kt = k_ref[...] * e_gi
    bt = b_ref[...] * e_gi
    kkt = kk_ref[...] * jnp.exp(gcum - lw)
    v = v_ref[...]

    row = lax.broadcasted_iota(jnp.int32, (2 * L, 2 * L), 0)
    col = lax.broadcasted_iota(jnp.int32, (2 * L, 2 * L), 1)
    same = (row // L) == (col // L)
    strict = same & ((col % L) < (row % L))
    incl = same & ((col % L) <= (row % L))
    head0 = lax.broadcasted_iota(jnp.int32, (L, 2 * L), 1) < RW_N

    def bd(x):
        return jnp.concatenate([jnp.where(head0, x, 0.0), jnp.where(head0, 0.0, x)], axis=0).astype(BF16)

    def dup(x):
        xb = x.astype(BF16)
        return jnp.concatenate([xb, xb], axis=0)

    for p in range(RW_H // 2):
        sl = slice(p * 2 * RW_N, (p + 1) * 2 * RW_N)
        kk_bd, r_bd = bd(kkt[:, sl]), bd(rt[:, sl])
        b_bd, k_bd = bd(bt[:, sl]), bd(kt[:, sl])
        b_dup, k_dup = dup(bt[:, sl]), dup(kt[:, sl])
        v_bd = bd(v[:, sl])
        sp = s_ref[p]
        sp_b = sp.astype(BF16)

        n1 = jnp.where(strict, _dot_nt(kk_bd, b_dup), 0.0)
        a_bk = jnp.where(strict, _dot_nt(kk_bd, k_dup), 0.0).astype(BF16)
        a_rb = jnp.where(incl, _dot_nt(r_bd, b_dup), 0.0).astype(BF16)
        a_rk = jnp.where(incl, _dot_nt(r_bd, k_dup), 0.0).astype(BF16)

        x = -(_dot_nt(kk_bd, sp_b) + _dot(a_bk, v_bd))
        n = n1
        x = x - _mm3(n, x)
        steps = 1
        while 2 * steps < L:
            n = _mm3(n, n)
            x = x + _mm3(n, x)
            steps *= 2
        d_b = x.astype(BF16)

        y_bd = _dot_nt(r_bd, sp_b) + _dot(a_rb, d_b) + _dot(a_rk, v_bd)
        y_ref[:, sl] = y_bd[:L] + y_bd[L:]
        upd = _dot_tn(d_b, b_bd) + _dot_tn(v_bd, k_bd)
        s_ref[p] = (sp + upd) * e_g[L - 1:L, sl]


def _mm3(a, b):
    a_hi, a_lo = _split(a)
    b_hi, b_lo = _split(b)
    return _dot(a_hi, b_hi) + _dot(a_hi, b_lo) + _dot(a_lo, b_hi)


def _wkv_chunked(r, lw, k, v, kk, b, s0_pairs):
    t = r.shape[0]
    row = lambda i: (i, 0)
    n_pairs = RW_H // 2
    spec = pl.BlockSpec((WKV_L, RW_W), row)
    sspec = pl.BlockSpec((n_pairs, LANE, LANE), lambda i: (0, 0, 0))
    return pl.pallas_call(
        _wkv_chunk_kernel,
        out_shape=(jax.ShapeDtypeStruct((t, RW_W), F32), jax.ShapeDtypeStruct((n_pairs, LANE, LANE), F32)),
        grid=(t // WKV_L,),
        in_specs=[spec] * 6 + [sspec],
        out_specs=(spec, sspec),
        compiler_params=_cparams(("arbitrary",)),
        name="wkv_chunk",
    )(r, lw, k, v, kk, b, s0_pairs)


def _wkv_step_kernel(s_ref, r_ref, lw_ref, k_ref, v_ref, kk_ref, b_ref, y_ref, so_ref):
    s = s_ref[0]
    s_kk = jnp.sum(s * kk_ref[0], axis=-1, keepdims=True)
    s = s * jnp.exp(lw_ref[0]) - s_kk * b_ref[0] + v_ref[0] * k_ref[0]
    so_ref[0] = s
    y_ref[0] = jnp.sum(s * r_ref[0], axis=-1, keepdims=True)


def _wkv_step(state, r, lw, k, v, kk, b):
    bsz = state.shape[0]
    keyed = lambda x: x.reshape(bsz, RW_H, 1, RW_N)
    kspec = pl.BlockSpec((1, RW_H, 1, RW_N), lambda i: (i, 0, 0, 0))
    vspec = pl.BlockSpec((1, RW_H, RW_N, 1), lambda i: (i, 0, 0, 0))
    sspec = pl.BlockSpec((1, RW_H, RW_N, RW_N), lambda i: (i, 0, 0, 0))
    y, s_new = pl.pallas_call(
        _wkv_step_kernel,
        out_shape=(jax.ShapeDtypeStruct((bsz, RW_H, RW_N, 1), F32), jax.ShapeDtypeStruct(state.shape, F32)),
        grid=(bsz,),
        in_specs=[sspec, kspec, kspec, kspec, vspec, kspec, kspec],
        out_specs=(vspec, sspec),
        compiler_params=_cparams(("parallel",)),
        name="wkv_step",
    )(state, keyed(r), keyed(lw), keyed(k), v.reshape(bsz, RW_H, RW_N, 1), keyed(kk), keyed(b))
    return y.reshape(bsz, RW_W), s_new


def _mix_out_kernel(x_ref, of_ref, y_ref, bonus_ref, g_ref, lnw_ref, lnb_ref, ind_ref, indt_ref, wo_ref, o_ref):
    y = y_ref[...]
    ind, ind_t = ind_ref[...], indt_ref[...]
    mean = _head_sum(y, ind, ind_t) * (1.0 / RW_N)
    yc = y - mean
    var = _head_sum(yc * yc, ind, ind_t) * (1.0 / RW_N)
    yn = yc * lax.rsqrt(var + LNX_EPS) * lnw_ref[...] + lnb_ref[...]
    o_rw = ((yn + bonus_ref[...]) * g_ref[...]).astype(BF16)
    o_ref[...] = (x_ref[...] + _dot(of_ref[...].astype(BF16), wo_ref[:FOX_W, :]) + _dot(o_rw, wo_ref[FOX_W:, :]))


def _mix_out(x, o_fox, y, bonus, g, lnw_row, lnb_row, ind, ind_t, w_out, tm):
    t, d = x.shape
    row = lambda i: (i, 0)
    const = lambda i: (0, 0)
    half = pl.BlockSpec((tm, RW_W), row)
    return pl.pallas_call(
        _mix_out_kernel,
        out_shape=jax.ShapeDtypeStruct((t, d), F32),
        grid=(t // tm,),
        in_specs=[
            pl.BlockSpec((tm, d), row), pl.BlockSpec((tm, FOX_W), row), half, half, half,
            pl.BlockSpec((1, RW_W), const), pl.BlockSpec((1, RW_W), const),
            pl.BlockSpec((RW_W, LANE), const), pl.BlockSpec((LANE, RW_W), const),
            pl.BlockSpec((d, d), const),
        ],
        out_specs=pl.BlockSpec((tm, d), row),
        compiler_params=_cparams(("parallel",)),
        name="mix_out",
    )(x, o_fox, y, bonus, g, lnw_row, lnb_row, ind, ind_t, w_out)


def _final_norm_kernel(x_ref, g_ref, o_ref):
    o_ref[...] = _rms(x_ref[...], g_ref[...])


def _final_norm(x, g, tm):
    t, d = x.shape
    return pl.pallas_call(
        _final_norm_kernel,
        out_shape=jax.ShapeDtypeStruct((t, d), F32),
        grid=(t // tm,),
        in_specs=[pl.BlockSpec((tm, d), lambda i: (i, 0)), pl.BlockSpec((1, d), lambda i: (0, 0))],
        out_specs=pl.BlockSpec((tm, d), lambda i: (i, 0)),
        compiler_params=_cparams(("parallel",)),
        name="final_norm",
    )(x, g)


def _pack_cols(a):
    lead = a.shape[:-1]
    z = lambda n: jnp.zeros(lead + (n,), a.dtype)
    return jnp.concatenate([a[..., :PR_WA + LANE], z(LANE), a[..., PR_WA + LANE:], z(2 * LANE - R_G)], axis=-1)


def _unpack_cols(a):
    return jnp.concatenate([a[..., :PR_WA + LANE], a[..., PR_G:PR_G + R_G]], axis=-1)


def _layer_weights(l, norm_ffa, ffa_w1, ffa_w3, ffa_w2, norm_mix, w_in, b_f, fox_gain, mu_shift, w0, w_up,
                   a0, a_up, g_up, k_k, k_a, r_k, lnx_w, lnx_b, w_out, norm_ffb, ffb_w1, ffb_w3, ffb_w2):
    d = D_MODEL
    w_rw = _pack_cols(w_in[l][:, C_FOX_IN:])
    w_rw = w_rw.at[:, PR_F:PR_F + FOX_H].set(w_in[l][:, 3 * FOX_W:C_FOX_IN])
    zeros = jnp.zeros((R_DECAY, RW_W), F32)
    w_wa = jnp.concatenate([jnp.concatenate([w_up[l], zeros], axis=1), jnp.concatenate([zeros, a_up[l]], axis=1)], axis=0)
    w_g = jnp.concatenate([g_up[l], jnp.zeros((2 * LANE - R_G, RW_W), F32)], axis=0)
    vecs = jnp.stack([w0[l], a0[l], k_k[l], k_a[l], r_k[l].reshape(RW_W)] + [jnp.zeros((RW_W,), F32)] * 3)
    return dict(
        norm_ffa=norm_ffa[l].reshape(1, d), ffa_w1=ffa_w1[l].astype(BF16), ffa_w3=ffa_w3[l].astype(BF16),
        ffa_w2=ffa_w2[l].astype(BF16),
        norm_mix=norm_mix[l].reshape(1, d), w_qkv=w_in[l][:, :3 * FOX_W].astype(BF16), w_rw=w_rw.astype(BF16),
        bf_row=jnp.pad(b_f[l], (0, LANE - FOX_H)).reshape(1, LANE),
        gain_row=fox_gain[l].reshape(1, FOX_W), mu_row=_pack_cols(mu_shift[l]).reshape(1, PR_W),
        vecs=vecs, w_wa=w_wa.astype(BF16), w_g=w_g.astype(BF16),
        lnw_row=lnx_w[l].reshape(1, RW_W), lnb_row=lnx_b[l].reshape(1, RW_W), w_out=w_out[l].astype(BF16),
        norm_ffb=norm_ffb[l].reshape(1, d), ffb_w1=ffb_w1[l].astype(BF16), ffb_w3=ffb_w3[l].astype(BF16),
        ffb_w2=ffb_w2[l].astype(BF16),
    )


def _head_indicators():
    lane_head = jnp.arange(RW_W) // RW_N
    ind = (lane_head[:, None] == jnp.arange(LANE)[None, :]).astype(BF16)
    return ind, ind.T


def _pairs_from_state(s):
    s = s.reshape(RW_H // 2, 2, RW_N, RW_N)
    z = jnp.zeros_like(s[:, 0])
    return jnp.concatenate([jnp.concatenate([s[:, 0], z], axis=2), jnp.concatenate([z, s[:, 1]], axis=2)], axis=1)


def _state_from_pairs(sp):
    return jnp.stack([sp[:, :RW_N, :RW_N], sp[:, RW_N:, RW_N:]], axis=1).reshape(RW_H, RW_N, RW_N)


def _tile(t, pref):
    return pref if t % pref == 0 else t


def _layer_prompt(x, lw_, ind, ind_t):
    t = x.shape[0]
    tm = _tile(t, 512)
    x = _ffn(x, lw_["norm_ffa"], lw_["ffa_w1"], lw_["ffa_w3"], lw_["ffa_w2"], tm, 512)
    qkv = _norm_matmul(x, lw_["norm_mix"], lw_["w_qkv"], tm, 1024)
    pr = _norm_matmul(x, lw_["norm_mix"], lw_["w_rw"], tm, PR_W // 4)
    logf, c, ct = _logf(pr, lw_["bf_row"], _tile(t, 256))
    o_fox = _fox_prompt(qkv, c, ct, lw_["gain_row"], tm)
    r, lw, k, v, kk, b, g, bonus = _rwkv_prep(pr, pr, lw_["mu_row"], lw_["vecs"], lw_["w_wa"], lw_["w_g"],
                                              ind, ind_t, _tile(t, 256), True)
    y, s_pairs = _wkv_chunked(r, lw, k, v, kk, b, jnp.zeros((RW_H // 2, LANE, LANE), F32))
    x = _mix_out(x, o_fox, y, bonus, g, lw_["lnw_row"], lw_["lnb_row"], ind, ind_t, lw_["w_out"], _tile(t, 256))
    x = _ffn(x, lw_["norm_ffb"], lw_["ffb_w1"], lw_["ffb_w3"], lw_["ffb_w2"], tm, 512)
    k_new = qkv[:, FOX_W:2 * FOX_W].reshape(1, t, FOX_H, FOX_HD)
    v_new = qkv[:, 2 * FOX_W:].reshape(1, t, FOX_H, FOX_HD)
    return (x, k_new, v_new, logf.reshape(1, t, FOX_H), _state_from_pairs(s_pairs)[None],
            _unpack_cols(pr[t - 1:t, :]))


def _layer_sample(x, lw_, ind, ind_t, layer, cache_k, cache_v, cache_ft, page_table, state, shift_prev):
    bsz = x.shape[0]
    x = _ffn(x, lw_["norm_ffa"], lw_["ffa_w1"], lw_["ffa_w3"], lw_["ffa_w2"], bsz, 1408)
    qkv = _norm_matmul(x, lw_["norm_mix"], lw_["w_qkv"], bsz, 1024)
    pr = _norm_matmul(x, lw_["norm_mix"], lw_["w_rw"], bsz, PR_W // 4)
    logf = _logf_only(pr, lw_["bf_row"])
    heads = lambda z: z.reshape(bsz, FOX_H, FOX_HD)
    k3, v3 = heads(qkv[:, FOX_W:2 * FOX_W]), heads(qkv[:, 2 * FOX_W:])
    o_fox = _fox_sample(page_table, heads(qkv[:, :FOX_W]), k3, v3, logf.reshape(bsz, FOX_H, 1), cache_k, cache_v,
                        cache_ft, lw_["gain_row"].reshape(FOX_H, FOX_HD), layer).reshape(bsz, FOX_W)
    r, lw, k, v, kk, b, g, bonus = _rwkv_prep(pr, _pack_cols(shift_prev), lw_["mu_row"], lw_["vecs"], lw_["w_wa"],
                                              lw_["w_g"], ind, ind_t, bsz, False)
    y, s_new = _wkv_step(state, r, lw, k, v, kk, b)
    x = _mix_out(x, o_fox, y, bonus, g, lw_["lnw_row"], lw_["lnb_row"], ind, ind_t, lw_["w_out"], bsz)
    x = _ffn(x, lw_["norm_ffb"], lw_["ffb_w1"], lw_["ffb_w3"], lw_["ffb_w2"], bsz, 1408)
    return (x, k3.reshape(bsz, 1, FOX_H, FOX_HD), v3.reshape(bsz, 1, FOX_H, FOX_HD), logf.reshape(bsz, 1, FOX_H),
            s_new, _unpack_cols(pr))


def kernel(x_prompt, x_sample, cache_k, cache_v, cache_logf, state_rwkv, state_shift, page_table, norm_ffa, ffa_w1, ffa_w3, ffa_w2, norm_mix, w_in, b_f, fox_gain, mu_shift, w0, w_up, a0, a_up, g_up, k_k, k_a, r_k, lnx_w, lnx_b, w_out, norm_ffb, ffb_w1, ffb_w3, ffb_w2, norm_final):
    depth = norm_ffa.shape[0]
    b_p, seq, d = x_prompt.shape
    b_s = x_sample.shape[0]
    assert b_p == 1 and x_sample.shape[1] == 1 and seq % WKV_L == 0
    cache_ft = jnp.swapaxes(cache_logf, 2, 3)
    ind, ind_t = _head_indicators()
    xp = x_prompt.reshape(seq, d)
    xs = x_sample.reshape(b_s, d)
    outs_p, outs_s = [], []
    for l in range(depth):
        lw_ = _layer_weights(l, norm_ffa, ffa_w1, ffa_w3, ffa_w2, norm_mix, w_in, b_f, fox_gain, mu_shift, w0, w_up,
                             a0, a_up, g_up, k_k, k_a, r_k, lnx_w, lnx_b, w_out, norm_ffb, ffb_w1, ffb_w3, ffb_w2)
        xp, *rest_p = _layer_prompt(xp, lw_, ind, ind_t)
        xs, *rest_s = _layer_sample(xs, lw_, ind, ind_t, l, cache_k, cache_v, cache_ft, page_table, state_rwkv[l],
                                    state_shift[l])
        outs_p.append(rest_p)
        outs_s.append(rest_s)
    g_final = norm_final.reshape(1, d)
    y_prompt = _final_norm(xp, g_final, _tile(seq, 512)).reshape(b_p, seq, d)
    y_sample = _final_norm(xs, g_final, b_s).reshape(b_s, 1, d)
    stack = lambda outs, i: jnp.stack([o[i] for o in outs])
    return (y_prompt, y_sample,
            stack(outs_p, 0), stack(outs_p, 1), stack(outs_p, 2), stack(outs_p, 3), stack(outs_p, 4),
            stack(outs_s, 0), stack(outs_s, 1), stack(outs_s, 2), stack(outs_s, 3), stack(outs_s, 4))
```

```python
import functools

import jax
import jax.numpy as jnp
from jax import lax
from jax.experimental import pallas as pl
from jax.experimental.pallas import tpu as pltpu

F32 = jnp.float32
BF16 = jnp.bfloat16

D_MODEL = 2048
FOX_HD = 128
FOX_W = D_MODEL // 2
FOX_H = FOX_W // FOX_HD
RW_N = 64
RW_W = D_MODEL - FOX_W
RW_H = RW_W // RW_N
R_DECAY = 64
R_A = 64
R_G = 160
PAGE_SIZE = 128
C_FOX_IN = 3 * FOX_W + FOX_H
C_SHIFT = 3 * RW_W + R_DECAY + R_A + R_G
RMS_EPS = 1e-6
LNX_EPS = 64e-5
FFN_RES = 0.5

LOG2E = 1.4426950408889634
Q_SCALE = FOX_HD ** -0.5 * LOG2E
LANE = 128
PR_WA = 3 * RW_W
PR_F = PR_WA + LANE
PR_G = PR_F + LANE
PR_W = PR_G + 2 * LANE
WKV_L = 64
HEAD_UNROLL = 4
VMEM_LIMIT = 56 * 1024 * 1024


def _cparams(sem):
    return pltpu.CompilerParams(dimension_semantics=sem, vmem_limit_bytes=VMEM_LIMIT)


def _dot(a, b):
    return jnp.dot(a, b, preferred_element_type=F32)


def _dot_nt(a, b):
    return lax.dot_general(a, b, (((1,), (1,)), ((), ())), preferred_element_type=F32)


def _dot_tn(a, b):
    return lax.dot_general(a, b, (((0,), (0,)), ((), ())), preferred_element_type=F32)


def _split(x):
    hi = x.astype(BF16)
    lo = (x - hi.astype(F32)).astype(BF16)
    return hi, lo


def _rms(x, g):
    return x * lax.rsqrt(jnp.mean(x * x, axis=-1, keepdims=True) + RMS_EPS) * g


def _ffn_kernel(x_ref, g_ref, w1_ref, w3_ref, w2_ref, o_ref, hn_ref):
    @pl.when(pl.program_id(1) == 0)
    def _():
        x = x_ref[...]
        hn_ref[...] = _rms(x, g_ref[...]).astype(BF16)
        o_ref[...] = x

    hn = hn_ref[...]
    h1 = _dot(hn, w1_ref[...])
    h3 = _dot(hn, w3_ref[...])
    a = (h1 * jax.nn.sigmoid(h1) * h3 * FFN_RES).astype(BF16)
    o_ref[...] += _dot(a, w2_ref[...])


def _ffn(x, g, w1, w3, w2, tm, tf):
    t, d = x.shape
    f = w1.shape[1]
    return pl.pallas_call(
        _ffn_kernel,
        out_shape=jax.ShapeDtypeStruct((t, d), F32),
        grid=(t // tm, f // tf),
        in_specs=[
            pl.BlockSpec((tm, d), lambda i, j: (i, 0)),
            pl.BlockSpec((1, d), lambda i, j: (0, 0)),
            pl.BlockSpec((d, tf), lambda i, j: (0, j)),
            pl.BlockSpec((d, tf), lambda i, j: (0, j)),
            pl.BlockSpec((tf, d), lambda i, j: (j, 0)),
        ],
        out_specs=pl.BlockSpec((tm, d), lambda i, j: (i, 0)),
        scratch_shapes=[pltpu.VMEM((tm, d), BF16)],
        compiler_params=_cparams(("parallel", "arbitrary")),
        name="ffn",
    )(x, g, w1, w3, w2)


def _norm_matmul_kernel(x_ref, g_ref, w_ref, o_ref, hn_ref):
    @pl.when(pl.program_id(1) == 0)
    def _():
        hn_ref[...] = _rms(x_ref[...], g_ref[...]).astype(BF16)

    o_ref[...] = _dot(hn_ref[...], w_ref[...])


def _norm_matmul(x, g, w, tm, tn):
    t, d = x.shape
    n = w.shape[1]
    return pl.pallas_call(
        _norm_matmul_kernel,
        out_shape=jax.ShapeDtypeStruct((t, n), F32),
        grid=(t // tm, n // tn),
        in_specs=[
            pl.BlockSpec((tm, d), lambda i, j: (i, 0)),
            pl.BlockSpec((1, d), lambda i, j: (0, 0)),
            pl.BlockSpec((d, tn), lambda i, j: (0, j)),
        ],
        out_specs=pl.BlockSpec((tm, tn), lambda i, j: (i, j)),
        scratch_shapes=[pltpu.VMEM((tm, d), BF16)],
        compiler_params=_cparams(("parallel", "arbitrary")),
        name="norm_matmul",
    )(x, g, w)


def _proj_qkv_kernel(x_ref, g_ref, w_ref, k_ref, v_ref, qkvb_ref, hn_ref):
    j = pl.program_id(1)

    @pl.when(j == 0)
    def _():
        hn_ref[...] = _rms(x_ref[...], g_ref[...]).astype(BF16)

    res = _dot(hn_ref[...], w_ref[...])

    def emit(head_ref):
        for h in range(FOX_H):
            head_ref[:, h, :] = res[:, h * FOX_HD:(h + 1) * FOX_HD]
        qkvb_ref[...] = res.astype(BF16)

    pl.when(j == 0)(lambda: emit(k_ref))
    pl.when(j == 1)(lambda: emit(v_ref))

    @pl.when(j == 2)
    def _():
        qkvb_ref[...] = (res * Q_SCALE).astype(BF16)


def _proj_qkv(x, g, w_kvq, tm):
    t, d = x.shape
    heads = jax.ShapeDtypeStruct((t, FOX_H, FOX_HD), F32)
    hspec = pl.BlockSpec((tm, FOX_H, FOX_HD), lambda i, j: (i, 0, 0))
    return pl.pallas_call(
        _proj_qkv_kernel,
        out_shape=(heads, heads, jax.ShapeDtypeStruct((t, 3 * FOX_W), BF16)),
        grid=(t // tm, 3),
        in_specs=[
            pl.BlockSpec((tm, d), lambda i, j: (i, 0)),
            pl.BlockSpec((1, d), lambda i, j: (0, 0)),
            pl.BlockSpec((d, FOX_W), lambda i, j: (0, j)),
        ],
        out_specs=(hspec, hspec, pl.BlockSpec((tm, FOX_W), lambda i, j: (i, j))),
        scratch_shapes=[pltpu.VMEM((tm, d), BF16)],
        compiler_params=_cparams(("parallel", "arbitrary")),
        name="proj_qkv",
    )(x, g, w_kvq)


def _log_sigmoid(z):
    return jnp.minimum(z, 0.0) - jnp.log1p(jnp.exp(-jnp.abs(z)))


def _logf_kernel(f_ref, bf_ref, tri_ref, lf_ref, c_ref, ct_ref, carry_ref):
    @pl.when(pl.program_id(0) == 0)
    def _():
        carry_ref[...] = jnp.zeros_like(carry_ref)

    lf = _log_sigmoid(f_ref[...] + bf_ref[...])
    hi, lo = _split(lf)
    c = _dot(tri_ref[...], hi) + _dot(tri_ref[...], lo) + carry_ref[...]
    carry_ref[...] = c[-1:, :]
    lf_ref[...] = lf[:, :FOX_H]
    c2 = c * LOG2E
    c_ref[...] = c2[:, :FOX_H]
    ct_ref[...] = c2.T[:FOX_H, :]


def _logf(pr, bf_row, tm):
    t = pr.shape[0]
    tri = (lax.broadcasted_iota(jnp.int32, (tm, tm), 0) >= lax.broadcasted_iota(jnp.int32, (tm, tm), 1)).astype(BF16)
    return pl.pallas_call(
        _logf_kernel,
        out_shape=(
            jax.ShapeDtypeStruct((t, FOX_H), F32),
            jax.ShapeDtypeStruct((t, FOX_H), F32),
            jax.ShapeDtypeStruct((FOX_H, t), F32),
        ),
        grid=(t // tm,),
        in_specs=[
            pl.BlockSpec((tm, LANE), lambda i: (i, PR_F // LANE)),
            pl.BlockSpec((1, LANE), lambda i: (0, 0)),
            pl.BlockSpec((tm, tm), lambda i: (0, 0)),
        ],
        out_specs=(
            pl.BlockSpec((tm, FOX_H), lambda i: (i, 0)),
            pl.BlockSpec((tm, FOX_H), lambda i: (i, 0)),
            pl.BlockSpec((FOX_H, tm), lambda i: (0, i)),
        ),
        scratch_shapes=[pltpu.VMEM((1, LANE), F32)],
        compiler_params=_cparams(("arbitrary",)),
        name="logf_cumsum",
    )(pr, bf_row, tri)


def _logf_only_kernel(f_ref, bf_ref, lf_ref):
    lf_ref[...] = _log_sigmoid(f_ref[...] + bf_ref[...])[:, :FOX_H]


def _logf_only(pr, bf_row):
    t = pr.shape[0]
    return pl.pallas_call(
        _logf_only_kernel,
        out_shape=jax.ShapeDtypeStruct((t, FOX_H), F32),
        grid=(1,),
        in_specs=[
            pl.BlockSpec((t, LANE), lambda i: (0, PR_F // LANE)),
            pl.BlockSpec((1, LANE), lambda i: (0, 0)),
        ],
        out_specs=pl.BlockSpec((t, FOX_H), lambda i: (0, 0)),
        name="logf",
    )(pr, bf_row)


def _fox_prompt_kernel(q_ref, k_ref, v_ref, cq_ref, ck_ref, gain_ref, o_ref, m_ref, l_ref, acc_ref, cqc_ref, *, tq):
    qi = pl.program_id(0)
    ki = pl.program_id(1)

    @pl.when(ki == 0)
    def _():
        m_ref[...] = jnp.full_like(m_ref, -jnp.inf)
        l_ref[...] = jnp.zeros_like(l_ref)
        acc_ref[...] = jnp.zeros_like(acc_ref)
        cq = cq_ref[...]
        for h in range(FOX_H):
            cqc_ref[h] = jnp.broadcast_to(cq[:, h:h + 1], (tq, LANE))

    def step(diagonal):
        tk = k_ref.shape[0]
        if diagonal:
            keep = (lax.broadcasted_iota(jnp.int32, (tq, tq), 0) >= lax.broadcasted_iota(jnp.int32, (tq, tq), 1))
        ones = jnp.ones((tk, LANE), BF16)

        def head(h, carry):
            sl = pl.ds(pl.multiple_of(h * FOX_HD, FOX_HD), FOX_HD)
            s = _dot_nt(q_ref[:, sl], k_ref[:, sl]) - ck_ref[pl.ds(h, 1), :]
            if diagonal:
                s = jnp.where(keep, s, -jnp.inf)
            cq = cqc_ref[h]
            m_prev = m_ref[h]
            m_new = jnp.maximum(m_prev, jnp.max(s, axis=-1, keepdims=True) + cq)
            shift = m_new - cq
            p = jnp.exp2(s - jnp.concatenate([shift] * (tk // LANE), axis=1)).astype(BF16)
            alpha = jnp.exp2(m_prev - m_new)
            pv = _dot(p, jnp.concatenate([v_ref[:, sl], ones], axis=1))
            l_ref[h] = alpha * l_ref[h] + pv[:, FOX_HD:]
            acc_ref[:, sl] = alpha * acc_ref[:, sl] + pv[:, :FOX_HD]
            m_ref[h] = m_new
            return carry

        lax.fori_loop(0, FOX_H, head, 0, unroll=HEAD_UNROLL)

    @pl.when(ki < qi)
    def _():
        step(False)

    @pl.when(ki == qi)
    def _():
        step(True)
        for h in range(FOX_H):
            sl = slice(h * FOX_HD, (h + 1) * FOX_HD)
            o = acc_ref[:, sl] / l_ref[h]
            o_ref[:, sl] = _rms(o, gain_ref[:, sl])


def _fox_prompt(qkvb, c, ct, gain_row, tq):
    t = qkvb.shape[0]
    n = t // tq
    stat = pltpu.VMEM((FOX_H, tq, LANE), F32)
    return pl.pallas_call(
        functools.partial(_fox_prompt_kernel, tq=tq),
        out_shape=jax.ShapeDtypeStruct((t, FOX_W), F32),
        grid=(n, n),
        in_specs=[
            pl.BlockSpec((tq, FOX_W), lambda i, j: (i, 2)),
            pl.BlockSpec((tq, FOX_W), lambda i, j: (jnp.minimum(i, j), 0)),
            pl.BlockSpec((tq, FOX_W), lambda i, j: (jnp.minimum(i, j), 1)),
            pl.BlockSpec((tq, FOX_H), lambda i, j: (i, 0)),
            pl.BlockSpec((FOX_H, tq), lambda i, j: (0, jnp.minimum(i, j))),
            pl.BlockSpec((1, FOX_W), lambda i, j: (0, 0)),
        ],
        out_specs=pl.BlockSpec((tq, FOX_W), lambda i, j: (i, 0)),
        scratch_shapes=[stat, stat, pltpu.VMEM((tq, FOX_W), F32), stat],
        compiler_params=_cparams(("parallel", "arbitrary")),
        name="fox_prompt",
    )(qkvb, qkvb, qkvb, c, ct, gain_row)


def _fox_sample_kernel(pt_ref, q_ref, kn_ref, vn_ref, fn_ref, *refs, group):
    del pt_ref
    kc_refs, vc_refs, fc_refs = refs[:group], refs[group:2 * group], refs[2 * group:3 * group]
    u2_ref, gain_ref, o_ref, m_ref, l_ref, acc_ref, carry_ref = refs[3 * group:]
    p_idx = pl.program_id(1)
    n_flat = PAGE_SIZE * FOX_H
    q = q_ref[0].astype(F32)

    @pl.when(p_idx == 0)
    def _():
        m_ref[...] = jnp.sum(q * kn_ref[0], axis=-1, keepdims=True)
        l_ref[...] = jnp.ones_like(l_ref)
        acc_ref[...] = vn_ref[0]
        carry_ref[...] = fn_ref[0]

    qb = q.astype(BF16)
    own = ((lax.broadcasted_iota(jnp.int32, (FOX_H, n_flat), 1) % FOX_H)
           == lax.broadcasted_iota(jnp.int32, (FOX_H, n_flat), 0))
    carry = carry_ref[...]
    scores = []
    for kc_ref, fc_ref in zip(kc_refs, fc_refs):
        kf = kc_ref[0].reshape(n_flat, FOX_HD).astype(BF16)
        f_page = fc_ref[0]
        f_hi, f_lo = _split(f_page)
        bias = carry + _dot(f_hi, u2_ref[...]) + _dot(f_lo, u2_ref[...])
        scores.append(jnp.where(own, _dot_nt(qb, kf) + bias * LOG2E, -jnp.inf))
        carry = carry + jnp.sum(f_page, axis=-1, keepdims=True)
    carry_ref[...] = carry

    m_prev = m_ref[...]
    m_new = m_prev
    for s in scores:
        m_new = jnp.maximum(m_new, jnp.max(s, axis=-1, keepdims=True))
    alpha = jnp.exp2(m_prev - m_new)
    l_new = alpha * l_ref[...]
    acc = alpha * acc_ref[...]
    for s, vc_ref in zip(scores, vc_refs):
        p = jnp.exp2(s - m_new)
        l_new = l_new + jnp.sum(p, axis=-1, keepdims=True)
        acc = acc + _dot(p.astype(BF16), vc_ref[0].reshape(n_flat, FOX_HD).astype(BF16))
    l_ref[...] = l_new
    acc_ref[...] = acc
    m_ref[...] = m_new

    @pl.when(p_idx == pl.num_programs(1) - 1)
    def _():
        o_ref[0] = _rms(acc / l_new, gain_ref[...])


def _fox_sample(page_table, q, k_new, v_new, f_new, cache_k, cache_v, cache_ft, gain, layer):
    b = q.shape[0]
    n_pages = page_table.shape[1]
    group = max(g for g in (8, 4, 2, 1) if n_pages % g == 0)
    last = n_pages - 1
    n_flat = PAGE_SIZE * FOX_H
    u2 = (jnp.arange(PAGE_SIZE)[:, None] > (jnp.arange(n_flat) // FOX_H)[None, :]).astype(BF16)
    tok = pl.BlockSpec((1, FOX_H, FOX_HD), lambda i, p, pt: (i, 0, 0))

    def page_kv(g):
        return pl.BlockSpec((None, 1, PAGE_SIZE, FOX_H, FOX_HD),
                            lambda i, p, pt: (layer, pt[i, last - (p * group + g)], 0, 0, 0))

    def page_f(g):
        return pl.BlockSpec((None, 1, FOX_H, PAGE_SIZE), lambda i, p, pt: (layer, pt[i, last - (p * group + g)], 0, 0))

    grid_spec = pltpu.PrefetchScalarGridSpec(
        num_scalar_prefetch=1,
        grid=(b, n_pages // group),
        in_specs=[
            tok, tok, tok,
            pl.BlockSpec((1, FOX_H, 1), lambda i, p, pt: (i, 0, 0)),
            *[page_kv(g) for g in range(group)], *[page_kv(g) for g in range(group)],
            *[page_f(g) for g in range(group)],
            pl.BlockSpec((PAGE_SIZE, n_flat), lambda i, p, pt: (0, 0)),
            pl.BlockSpec((FOX_H, FOX_HD), lambda i, p, pt: (0, 0)),
        ],
        out_specs=tok,
        scratch_shapes=[
            pltpu.VMEM((FOX_H, 1), F32),
            pltpu.VMEM((FOX_H, 1), F32),
            pltpu.VMEM((FOX_H, FOX_HD), F32),
            pltpu.VMEM((FOX_H, 1), F32),
        ],
    )
    return pl.pallas_call(
        functools.partial(_fox_sample_kernel, group=group),
        out_shape=jax.ShapeDtypeStruct((b, FOX_H, FOX_HD), F32),
        grid_spec=grid_spec,
        compiler_params=_cparams(("parallel", "arbitrary")),
        name="fox_sample",
    )(page_table, q, k_new, v_new, f_new, *([cache_k] * group), *([cache_v] * group), *([cache_ft] * group), u2, gain)


def _head_sum(x, ind, ind_t):
    hi, lo = _split(x)
    s = _dot(hi, ind) + _dot(lo, ind)
    s_hi, s_lo = _split(s)
    return _dot(s_hi, ind_t) + _dot(s_lo, ind_t)


def _rwkv_prep_kernel(pr_ref, prev_ref, mu_ref, vec_ref, wwa_ref, wg_ref, ind_ref, indt_ref,
                      r_ref, lw_ref, k_ref, v_ref, kk_ref, b_ref, g_ref, bonus_ref, carry_ref, *, sequential):
    pr = pr_ref[...]
    if sequential:
        @pl.when(pl.program_id(0) == 0)
        def _():
            carry_ref[...] = jnp.zeros_like(carry_ref)

        rolled = pltpu.roll(pr, 1, 0)
        first = lax.broadcasted_iota(jnp.int32, pr.shape, 0) == 0
        prev = jnp.where(first, carry_ref[...], rolled)
        carry_ref[...] = pr[-1:, :]
    else:
        prev = prev_ref[...]
    xs = pr + (prev - pr) * mu_ref[...]
    r = xs[:, 0:RW_W]
    kr = xs[:, RW_W:2 * RW_W]
    vr = xs[:, 2 * RW_W:3 * RW_W]
    wa = xs[:, PR_WA:PR_WA + LANE]
    gd = xs[:, PR_G:PR_W]
    w0, a0, k_k, k_a, r_k = (vec_ref[i:i + 1, :] for i in range(5))

    is_decay = lax.broadcasted_iota(jnp.int32, wa.shape, 1) < R_DECAY
    wa_act = jnp.where(is_decay, jnp.tanh(wa), wa).astype(BF16)
    lora = _dot(wa_act, wwa_ref[...])
    x_w = w0 + lora[:, :RW_W]
    w_log = -(jnp.maximum(-x_w, 0.0) + jnp.log1p(jnp.exp(-jnp.abs(x_w)))) - 0.5
    lw_ref[...] = -jnp.exp(w_log)
    a = jax.nn.sigmoid(a0 + lora[:, RW_W:])
    g_ref[...] = _dot(jax.nn.sigmoid(gd).astype(BF16), wg_ref[...])

    kk = kr * k_k
    ss = _head_sum(kk * kk, ind_ref[...], indt_ref[...])
    kk = kk / jnp.maximum(jnp.sqrt(ss), 1e-12)
    k2 = kr * (1.0 + (a - 1.0) * k_a)
    bonus_ref[...] = _head_sum(r * k2 * r_k, ind_ref[...], indt_ref[...]) * vr
    r_ref[...] = r
    k_ref[...] = k2
    v_ref[...] = vr
    kk_ref[...] = kk
    b_ref[...] = kk * a


def _rwkv_prep(pr, prev, mu_row, vecs, w_wa, w_g, ind, ind_t, tm, sequential):
    t = pr.shape[0]
    row = lambda i: (i, 0)
    const = lambda i: (0, 0)
    out = jax.ShapeDtypeStruct((t, RW_W), F32)
    return pl.pallas_call(
        functools.partial(_rwkv_prep_kernel, sequential=sequential),
        out_shape=(out,) * 8,
        grid=(t // tm,),
        in_specs=[
            pl.BlockSpec((tm, PR_W), row),
            pl.BlockSpec((prev.shape[0] if sequential else tm, PR_W), const if sequential else row),
            pl.BlockSpec((1, PR_W), const),
            pl.BlockSpec((8, RW_W), const),
            pl.BlockSpec((LANE, 2 * RW_W), const),
            pl.BlockSpec((2 * LANE, RW_W), const),
            pl.BlockSpec((RW_W, LANE), const),
            pl.BlockSpec((LANE, RW_W), const),
        ],
        out_specs=(pl.BlockSpec((tm, RW_W), row),) * 8,
        scratch_shapes=[pltpu.VMEM((1, PR_W), F32)],
        compiler_params=_cparams(("arbitrary",)),
        name="rwkv_prep",
    )(pr, prev, mu_row, vecs, w_wa, w_g, ind, ind_t)


def _wkv_chunk_kernel(r_ref, lw_ref, k_ref, v_ref, kk_ref, b_ref, s0_ref, y_ref, s_ref):
    L = WKV_L

    @pl.when(pl.program_id(0) == 0)
    def _():
        s_ref[...] = s0_ref[...]

    lw = lw_ref[...]
    tril = (lax.broadcasted_iota(jnp.int32, (L, L), 0) >= lax.broadcasted_iota(jnp.int32, (L, L), 1)).astype(BF16)
    lw_hi, lw_lo = _split(lw)
    gcum = _dot(tril, lw_hi) + _dot(tril, lw_lo)
    e_g = jnp.exp(gcum)
    e_gi = jnp.exp(-gcum)
    rt = r_ref[...] * e_g
    kt = k_ref[...] * e_gi
    bt = b_ref[...] * e_gi
    kkt = kk_ref[...] * jnp.exp(gcum - lw)
    v = v_ref[...]

    row = lax.broadcasted_iota(jnp.int32, (2 * L, 2 * L), 0)
    col = lax.broadcasted_iota(jnp.int32, (2 * L, 2 * L), 1)
    same = (row // L) == (col // L)
    strict = same & ((col % L) < (row % L))
    incl = same & ((col % L) <= (row % L))
    head0 = lax.broadcasted_iota(jnp.int32, (L, 2 * L), 1) < RW_N

    def bd(x):
        return jnp.concatenate([jnp.where(head0, x, 0.0), jnp.where(head0, 0.0, x)], axis=0).astype(BF16)

    def dup(x):
        xb = x.astype(BF16)
        return jnp.concatenate([xb, xb], axis=0)

    pairs = range(RW_H // 2)
    sls = [slice(p * 2 * RW_N, (p + 1) * 2 * RW_N) for p in pairs]
    kkr_bd = [jnp.concatenate([bd(kkt[:, sl]), bd(rt[:, sl])], axis=0) for sl in sls]
    bk_bd = [jnp.concatenate([bd(bt[:, sl]), bd(kt[:, sl])], axis=0) for sl in sls]
    bk_dup = [jnp.concatenate([dup(bt[:, sl]), dup(kt[:, sl])], axis=0) for sl in sls]
    v_bd = [bd(v[:, sl]) for sl in sls]
    sp = [s_ref[p] for p in pairs]
    a_all = [_dot_nt(kkr_bd[p], bk_dup[p]) for p in pairs]
    from_state = [_dot_nt(kkr_bd[p], sp[p].astype(BF16)) for p in pairs]
    n = [jnp.where(strict, a[:2 * L, :2 * L], 0.0) for a in a_all]
    a_bk = [jnp.where(strict, a[:2 * L, 2 * L:], 0.0).astype(BF16) for a in a_all]
    a_r = [jnp.concatenate([jnp.where(incl, a[2 * L:, :2 * L], 0.0), jnp.where(incl, a[2 * L:, 2 * L:], 0.0)],
                           axis=1).astype(BF16) for a in a_all]
    x = [-(from_state[p][:2 * L] + _dot(a_bk[p], v_bd[p])) for p in pairs]
    x = [x[p] - _mm(n[p], x[p]) for p in pairs]
    steps = 1
    while 2 * steps < L:
        n = [_mm(n[p], n[p]) for p in pairs]
        x = [x[p] + _mm(n[p], x[p]) for p in pairs]
        steps *= 2
    dv = [jnp.concatenate([x[p].astype(BF16), v_bd[p]], axis=0) for p in pairs]
    for p in pairs:
        y_bd = from_state[p][2 * L:] + _dot(a_r[p], dv[p])
        y_ref[:, sls[p]] = y_bd[:L] + y_bd[L:]
        s_ref[p] = (sp[p] + _dot_tn(dv[p], bk_bd[p])) * e_g[L - 1:L, sls[p]]


def _mm(a, b):
    return _dot(a.astype(BF16), b.astype(BF16))


def _wkv_chunked(r, lw, k, v, kk, b, s0_pairs):
    t = r.shape[0]
    row = lambda i: (i, 0)
    n_pairs = RW_H // 2
    spec = pl.BlockSpec((WKV_L, RW_W), row)
    sspec = pl.BlockSpec((n_pairs, LANE, LANE), lambda i: (0, 0, 0))
    return pl.pallas_call(
        _wkv_chunk_kernel,
        out_shape=(jax.ShapeDtypeStruct((t, RW_W), F32), jax.ShapeDtypeStruct((n_pairs, LANE, LANE), F32)),
        grid=(t // WKV_L,),
        in_specs=[spec] * 6 + [sspec],
        out_specs=(spec, sspec),
        compiler_params=_cparams(("arbitrary",)),
        name="wkv_chunk",
    )(r, lw, k, v, kk, b, s0_pairs)


def _wkv_step_kernel(s_ref, r_ref, lw_ref, k_ref, v_ref, kk_ref, b_ref, y_ref, so_ref):
    s = s_ref[0]
    s_kk = jnp.sum(s * kk_ref[0], axis=-1, keepdims=True)
    s = s * jnp.exp(lw_ref[0]) - s_kk * b_ref[0] + v_ref[0] * k_ref[0]
    so_ref[0] = s
    y_ref[0] = jnp.sum(s * r_ref[0], axis=-1, keepdims=True)


def _wkv_step(state, r, lw, k, v, kk, b):
    bsz = state.shape[0]
    keyed = lambda x: x.reshape(bsz, RW_H, 1, RW_N)
    kspec = pl.BlockSpec((1, RW_H, 1, RW_N), lambda i: (i, 0, 0, 0))
    vspec = pl.BlockSpec((1, RW_H, RW_N, 1), lambda i: (i, 0, 0, 0))
    sspec = pl.BlockSpec((1, RW_H, RW_N, RW_N), lambda i: (i, 0, 0, 0))
    y, s_new = pl.pallas_call(
        _wkv_step_kernel,
        out_shape=(jax.ShapeDtypeStruct((bsz, RW_H, RW_N, 1), F32), jax.ShapeDtypeStruct(state.shape, F32)),
        grid=(bsz,),
        in_specs=[sspec, kspec, kspec, kspec, vspec, kspec, kspec],
        out_specs=(vspec, sspec),
        compiler_params=_cparams(("parallel",)),
        name="wkv_step",
    )(state, keyed(r), keyed(lw), keyed(k), v.reshape(bsz, RW_H, RW_N, 1), keyed(kk), keyed(b))
    return y.reshape(bsz, RW_W), s_new


def _mix_out_kernel(x_ref, of_ref, y_ref, bonus_ref, g_ref, lnw_ref, lnb_ref, ind_ref, indt_ref, wo_ref, o_ref):
    y = y_ref[...]
    ind, ind_t = ind_ref[...], indt_ref[...]
    mean = _head_sum(y, ind, ind_t) * (1.0 / RW_N)
    yc = y - mean
    var = _head_sum(yc * yc, ind, ind_t) * (1.0 / RW_N)
    yn = yc * lax.rsqrt(var + LNX_EPS) * lnw_ref[...] + lnb_ref[...]
    o_rw = ((yn + bonus_ref[...]) * g_ref[...]).astype(BF16)
    o_ref[...] = (x_ref[...] + _dot(of_ref[...].astype(BF16), wo_ref[:FOX_W, :]) + _dot(o_rw, wo_ref[FOX_W:, :]))


def _mix_out(x, o_fox, y, bonus, g, lnw_row, lnb_row, ind, ind_t, w_out, tm):
    t, d = x.shape
    row = lambda i: (i, 0)
    const = lambda i: (0, 0)
    half = pl.BlockSpec((tm, RW_W), row)
    return pl.pallas_call(
        _mix_out_kernel,
        out_shape=jax.ShapeDtypeStruct((t, d), F32),
        grid=(t // tm,),
        in_specs=[
            pl.BlockSpec((tm, d), row), pl.BlockSpec((tm, FOX_W), row), half, half, half,
            pl.BlockSpec((1, RW_W), const), pl.BlockSpec((1, RW_W), const),
            pl.BlockSpec((RW_W, LANE), const), pl.BlockSpec((LANE, RW_W), const),
            pl.BlockSpec((d, d), const),
        ],
        out_specs=pl.BlockSpec((tm, d), row),
        compiler_params=_cparams(("parallel",)),
        name="mix_out",
    )(x, o_fox, y, bonus, g, lnw_row, lnb_row, ind, ind_t, w_out)


def _final_norm_kernel(x_ref, g_ref, o_ref):
    o_ref[...] = _rms(x_ref[...], g_ref[...])


def _final_norm(x, g, tm):
    t, d = x.shape
    return pl.pallas_call(
        _final_norm_kernel,
        out_shape=jax.ShapeDtypeStruct((t, d), F32),
        grid=(t // tm,),
        in_specs=[pl.BlockSpec((tm, d), lambda i: (i, 0)), pl.BlockSpec((1, d), lambda i: (0, 0))],
        out_specs=pl.BlockSpec((tm, d), lambda i: (i, 0)),
        compiler_params=_cparams(("parallel",)),
        name="final_norm",
    )(x, g)


def _pack_cols(a):
    lead = a.shape[:-1]
    z = lambda n: jnp.zeros(lead + (n,), a.dtype)
    return jnp.concatenate([a[..., :PR_WA + LANE], z(LANE), a[..., PR_WA + LANE:], z(2 * LANE - R_G)], axis=-1)


def _unpack_cols(a):
    return jnp.concatenate([a[..., :PR_WA + LANE], a[..., PR_G:PR_G + R_G]], axis=-1)


def _layer_weights(l, norm_ffa, ffa_w1, ffa_w3, ffa_w2, norm_mix, w_in, b_f, fox_gain, mu_shift, w0, w_up,
                   a0, a_up, g_up, k_k, k_a, r_k, lnx_w, lnx_b, w_out, norm_ffb, ffb_w1, ffb_w3, ffb_w2):
    d = D_MODEL
    w_rw = _pack_cols(w_in[l][:, C_FOX_IN:])
    w_rw = w_rw.at[:, PR_F:PR_F + FOX_H].set(w_in[l][:, 3 * FOX_W:C_FOX_IN])
    zeros = jnp.zeros((R_DECAY, RW_W), F32)
    w_wa = jnp.concatenate([jnp.concatenate([w_up[l], zeros], axis=1), jnp.concatenate([zeros, a_up[l]], axis=1)], axis=0)
    w_g = jnp.concatenate([g_up[l], jnp.zeros((2 * LANE - R_G, RW_W), F32)], axis=0)
    vecs = jnp.stack([w0[l], a0[l], k_k[l], k_a[l], r_k[l].reshape(RW_W)] + [jnp.zeros((RW_W,), F32)] * 3)
    return dict(
        norm_ffa=norm_ffa[l].reshape(1, d), ffa_w1=ffa_w1[l].astype(BF16), ffa_w3=ffa_w3[l].astype(BF16),
        ffa_w2=ffa_w2[l].astype(BF16),
        norm_mix=norm_mix[l].reshape(1, d), w_rw=w_rw.astype(BF16),
        w_kvq=jnp.concatenate([w_in[l][:, FOX_W:3 * FOX_W], w_in[l][:, :FOX_W]], axis=1).astype(BF16),
        bf_row=jnp.pad(b_f[l], (0, LANE - FOX_H)).reshape(1, LANE),
        gain_row=fox_gain[l].reshape(1, FOX_W), mu_row=_pack_cols(mu_shift[l]).reshape(1, PR_W),
        vecs=vecs, w_wa=w_wa.astype(BF16), w_g=w_g.astype(BF16),
        lnw_row=lnx_w[l].reshape(1, RW_W), lnb_row=lnx_b[l].reshape(1, RW_W), w_out=w_out[l].astype(BF16),
        norm_ffb=norm_ffb[l].reshape(1, d), ffb_w1=ffb_w1[l].astype(BF16), ffb_w3=ffb_w3[l].astype(BF16),
        ffb_w2=ffb_w2[l].astype(BF16),
    )


def _head_indicators():
    lane_head = jnp.arange(RW_W) // RW_N
    ind = (lane_head[:, None] == jnp.arange(LANE)[None, :]).astype(BF16)
    return ind, ind.T


def _pairs_from_state(s):
    s = s.reshape(RW_H // 2, 2, RW_N, RW_N)
    z = jnp.zeros_like(s[:, 0])
    return jnp.concatenate([jnp.concatenate([s[:, 0], z], axis=2), jnp.concatenate([z, s[:, 1]], axis=2)], axis=1)


def _state_from_pairs(sp):
    return jnp.stack([sp[:, :RW_N, :RW_N], sp[:, RW_N:, RW_N:]], axis=1).reshape(RW_H, RW_N, RW_N)


def _tile(t, pref):
    return pref if t % pref == 0 else t


def _layer_prompt(x, lw_, ind, ind_t):
    t = x.shape[0]
    tm = _tile(t, 512)
    x = _ffn(x, lw_["norm_ffa"], lw_["ffa_w1"], lw_["ffa_w3"], lw_["ffa_w2"], tm, 512)
    k_new, v_new, qkvb = _proj_qkv(x, lw_["norm_mix"], lw_["w_kvq"], tm)
    pr = _norm_matmul(x, lw_["norm_mix"], lw_["w_rw"], tm, PR_W // 4)
    logf, c, ct = _logf(pr, lw_["bf_row"], _tile(t, 256))
    o_fox = _fox_prompt(qkvb, c, ct, lw_["gain_row"], tm)
    r, lw, k, v, kk, b, g, bonus = _rwkv_prep(pr, lw_["mu_row"], lw_["mu_row"], lw_["vecs"], lw_["w_wa"], lw_["w_g"],
                                              ind, ind_t, _tile(t, 256), True)
    y, s_pairs = _wkv_chunked(r, lw, k, v, kk, b, jnp.zeros((RW_H // 2, LANE, LANE), F32))
    x = _mix_out(x, o_fox, y, bonus, g, lw_["lnw_row"], lw_["lnb_row"], ind, ind_t, lw_["w_out"], _tile(t, 256))
    x = _ffn(x, lw_["norm_ffb"], lw_["ffb_w1"], lw_["ffb_w3"], lw_["ffb_w2"], tm, 512)
    return (x, k_new[None], v_new[None], logf.reshape(1, t, FOX_H), _state_from_pairs(s_pairs)[None],
            _unpack_cols(pr[t - 1:t, :]))


def _layer_sample(x, lw_, ind, ind_t, layer, cache_k, cache_v, cache_ft, page_table, state, shift_prev):
    bsz = x.shape[0]
    x = _ffn(x, lw_["norm_ffa"], lw_["ffa_w1"], lw_["ffa_w3"], lw_["ffa_w2"], bsz, 1408)
    k3, v3, qkvb = _proj_qkv(x, lw_["norm_mix"], lw_["w_kvq"], bsz)
    pr = _norm_matmul(x, lw_["norm_mix"], lw_["w_rw"], bsz, PR_W // 4)
    logf = _logf_only(pr, lw_["bf_row"])
    q3 = qkvb[:, 2 * FOX_W:].reshape(bsz, FOX_H, FOX_HD)
    o_fox = _fox_sample(page_table, q3, k3, v3, logf.reshape(bsz, FOX_H, 1), cache_k, cache_v,
                        cache_ft, lw_["gain_row"].reshape(FOX_H, FOX_HD), layer).reshape(bsz, FOX_W)
    r, lw, k, v, kk, b, g, bonus = _rwkv_prep(pr, _pack_cols(shift_prev), lw_["mu_row"], lw_["vecs"], lw_["w_wa"],
                                              lw_["w_g"], ind, ind_t, bsz, False)
    y, s_new = _wkv_step(state, r, lw, k, v, kk, b)
    x = _mix_out(x, o_fox, y, bonus, g, lw_["lnw_row"], lw_["lnb_row"], ind, ind_t, lw_["w_out"], bsz)
    x = _ffn(x, lw_["norm_ffb"], lw_["ffb_w1"], lw_["ffb_w3"], lw_["ffb_w2"], bsz, 1408)
    return (x, k3.reshape(bsz, 1, FOX_H, FOX_HD), v3.reshape(bsz, 1, FOX_H, FOX_HD), logf.reshape(bsz, 1, FOX_H),
            s_new, _unpack_cols(pr))


def kernel(x_prompt, x_sample, cache_k, cache_v, cache_logf, state_rwkv, state_shift, page_table, norm_ffa, ffa_w1, ffa_w3, ffa_w2, norm_mix, w_in, b_f, fox_gain, mu_shift, w0, w_up, a0, a_up, g_up, k_k, k_a, r_k, lnx_w, lnx_b, w_out, norm_ffb, ffb_w1, ffb_w3, ffb_w2, norm_final):
    depth = norm_ffa.shape[0]
    b_p, seq, d = x_prompt.shape
    b_s = x_sample.shape[0]
    assert b_p == 1 and x_sample.shape[1] == 1 and seq % WKV_L == 0
    cache_ft = jnp.swapaxes(cache_logf, 2, 3)
    ind, ind_t = _head_indicators()
    xp = x_prompt.reshape(seq, d)
    xs = x_sample.reshape(b_s, d)
    outs_p, outs_s = [], []
    for l in range(depth):
        lw_ = _layer_weights(l, norm_ffa, ffa_w1, ffa_w3, ffa_w2, norm_mix, w_in, b_f, fox_gain, mu_shift, w0, w_up,
                             a0, a_up, g_up, k_k, k_a, r_k, lnx_w, lnx_b, w_out, norm_ffb, ffb_w1, ffb_w3, ffb_w2)
        xp, *rest_p = _layer_prompt(xp, lw_, ind, ind_t)
        xs, *rest_s = _layer_sample(xs, lw_, ind, ind_t, l, cache_k, cache_v, cache_ft, page_table, state_rwkv[l],
                                    state_shift[l])
        outs_p.append(rest_p)
        outs_s.append(rest_s)
    g_final = norm_final.reshape(1, d)
    y_prompt = _final_norm(xp, g_final, _tile(seq, 512)).reshape(b_p, seq, d)
    y_sample = _final_norm(xs, g_final, b_s).reshape(b_s, 1, d)
    stack = lambda outs, i: jnp.stack([o[i] for o in outs])
    return (y_prompt, y_sample,
            stack(outs_p, 0), stack(outs_p, 1), stack(outs_p, 2), stack(outs_p, 3), stack(outs_p, 4),
            stack(outs_s, 0), stack(outs_s, 1), stack(outs_s, 2), stack(outs_s, 3), stack(outs_s, 4))
```

```python
import functools

import jax
import jax.numpy as jnp
from jax import lax
from jax.experimental import pallas as pl
from jax.experimental.pallas import tpu as pltpu

F32 = jnp.float32
BF16 = jnp.bfloat16

D_MODEL = 2048
FOX_HD = 128
FOX_W = D_MODEL // 2
FOX_H = FOX_W // FOX_HD
RW_N = 64
RW_W = D_MODEL - FOX_W
RW_H = RW_W // RW_N
R_DECAY = 64
R_A = 64
R_G = 160
PAGE_SIZE = 128
C_FOX_IN = 3 * FOX_W + FOX_H
C_SHIFT = 3 * RW_W + R_DECAY + R_A + R_G
RMS_EPS = 1e-6
LNX_EPS = 64e-5
FFN_RES = 0.5

LOG2E = 1.4426950408889634
Q_SCALE = FOX_HD ** -0.5 * LOG2E
LANE = 128
PR_WA = 3 * RW_W
PR_F = PR_WA + LANE
PR_G = PR_F + LANE
PR_W = PR_G + 2 * LANE
WKV_L = 64
HEAD_UNROLL = 4
VMEM_LIMIT = 56 * 1024 * 1024


def _cparams(sem):
    return pltpu.CompilerParams(dimension_semantics=sem, vmem_limit_bytes=VMEM_LIMIT)


def _dot(a, b):
    return jnp.dot(a, b, preferred_element_type=F32)


def _dot_nt(a, b):
    return lax.dot_general(a, b, (((1,), (1,)), ((), ())), preferred_element_type=F32)


def _dot_tn(a, b):
    return lax.dot_general(a, b, (((0,), (0,)), ((), ())), preferred_element_type=F32)


def _split(x):
    hi = x.astype(BF16)
    lo = (x - hi.astype(F32)).astype(BF16)
    return hi, lo


def _rms(x, g):
    return x * lax.rsqrt(jnp.mean(x * x, axis=-1, keepdims=True) + RMS_EPS) * g


def _ffn_kernel(x_ref, g_ref, w1_ref, w3_ref, w2_ref, o_ref, hn_ref):
    @pl.when(pl.program_id(1) == 0)
    def _():
        x = x_ref[...]
        hn_ref[...] = _rms(x, g_ref[...]).astype(BF16)
        o_ref[...] = x

    hn = hn_ref[...]
    h1 = _dot(hn, w1_ref[...])
    h3 = _dot(hn, w3_ref[...])
    a = (h1 * jax.nn.sigmoid(h1) * h3 * FFN_RES).astype(BF16)
    o_ref[...] += _dot(a, w2_ref[...])


def _ffn(x, g, w1, w3, w2, tm, tf):
    t, d = x.shape
    f = w1.shape[1]
    return pl.pallas_call(
        _ffn_kernel,
        out_shape=jax.ShapeDtypeStruct((t, d), F32),
        grid=(t // tm, f // tf),
        in_specs=[
            pl.BlockSpec((tm, d), lambda i, j: (i, 0)),
            pl.BlockSpec((1, d), lambda i, j: (0, 0)),
            pl.BlockSpec((d, tf), lambda i, j: (0, j)),
            pl.BlockSpec((d, tf), lambda i, j: (0, j)),
            pl.BlockSpec((tf, d), lambda i, j: (j, 0)),
        ],
        out_specs=pl.BlockSpec((tm, d), lambda i, j: (i, 0)),
        scratch_shapes=[pltpu.VMEM((tm, d), BF16)],
        compiler_params=_cparams(("parallel", "arbitrary")),
        name="ffn",
    )(x, g, w1, w3, w2)


def _norm_matmul_kernel(x_ref, g_ref, w_ref, o_ref, hn_ref):
    @pl.when(pl.program_id(1) == 0)
    def _():
        hn_ref[...] = _rms(x_ref[...], g_ref[...]).astype(BF16)

    o_ref[...] = _dot(hn_ref[...], w_ref[...])


def _norm_matmul(x, g, w, tm, tn):
    t, d = x.shape
    n = w.shape[1]
    return pl.pallas_call(
        _norm_matmul_kernel,
        out_shape=jax.ShapeDtypeStruct((t, n), F32),
        grid=(t // tm, n // tn),
        in_specs=[
            pl.BlockSpec((tm, d), lambda i, j: (i, 0)),
            pl.BlockSpec((1, d), lambda i, j: (0, 0)),
            pl.BlockSpec((d, tn), lambda i, j: (0, j)),
        ],
        out_specs=pl.BlockSpec((tm, tn), lambda i, j: (i, j)),
        scratch_shapes=[pltpu.VMEM((tm, d), BF16)],
        compiler_params=_cparams(("parallel", "arbitrary")),
        name="norm_matmul",
    )(x, g, w)


def _proj_qkv_kernel(x_ref, g_ref, w_ref, k_ref, v_ref, qkvb_ref, hn_ref):
    j = pl.program_id(1)

    @pl.when(j == 0)
    def _():
        hn_ref[...] = _rms(x_ref[...], g_ref[...]).astype(BF16)

    res = _dot(hn_ref[...], w_ref[...])

    def emit(head_ref):
        for h in range(FOX_H):
            head_ref[:, h, :] = res[:, h * FOX_HD:(h + 1) * FOX_HD]
        qkvb_ref[...] = res.astype(BF16)

    pl.when(j == 0)(lambda: emit(k_ref))
    pl.when(j == 1)(lambda: emit(v_ref))

    @pl.when(j == 2)
    def _():
        qkvb_ref[...] = (res * Q_SCALE).astype(BF16)


def _proj_qkv(x, g, w_kvq, tm):
    t, d = x.shape
    heads = jax.ShapeDtypeStruct((t, FOX_H, FOX_HD), F32)
    hspec = pl.BlockSpec((tm, FOX_H, FOX_HD), lambda i, j: (i, 0, 0))
    return pl.pallas_call(
        _proj_qkv_kernel,
        out_shape=(heads, heads, jax.ShapeDtypeStruct((t, 3 * FOX_W), BF16)),
        grid=(t // tm, 3),
        in_specs=[
            pl.BlockSpec((tm, d), lambda i, j: (i, 0)),
            pl.BlockSpec((1, d), lambda i, j: (0, 0)),
            pl.BlockSpec((d, FOX_W), lambda i, j: (0, j)),
        ],
        out_specs=(hspec, hspec, pl.BlockSpec((tm, FOX_W), lambda i, j: (i, j))),
        scratch_shapes=[pltpu.VMEM((tm, d), BF16)],
        compiler_params=_cparams(("parallel", "arbitrary")),
        name="proj_qkv",
    )(x, g, w_kvq)


def _log_sigmoid(z):
    return jnp.minimum(z, 0.0) - jnp.log1p(jnp.exp(-jnp.abs(z)))


def _logf_kernel(f_ref, bf_ref, tri_ref, lf_ref, c_ref, ct_ref, carry_ref):
    @pl.when(pl.program_id(0) == 0)
    def _():
        carry_ref[...] = jnp.zeros_like(carry_ref)

    lf = _log_sigmoid(f_ref[...] + bf_ref[...])
    hi, lo = _split(lf)
    c = _dot(tri_ref[...], hi) + _dot(tri_ref[...], lo) + carry_ref[...]
    carry_ref[...] = c[-1:, :]
    lf_ref[...] = lf[:, :FOX_H]
    c2 = c * LOG2E
    c_ref[...] = c2[:, :FOX_H]
    ct_ref[...] = c2.T[:FOX_H, :]


def _logf(pr, bf_row, tm):
    t = pr.shape[0]
    tri = (lax.broadcasted_iota(jnp.int32, (tm, tm), 0) >= lax.broadcasted_iota(jnp.int32, (tm, tm), 1)).astype(BF16)
    return pl.pallas_call(
        _logf_kernel,
        out_shape=(
            jax.ShapeDtypeStruct((t, FOX_H), F32),
            jax.ShapeDtypeStruct((t, FOX_H), F32),
            jax.ShapeDtypeStruct((FOX_H, t), F32),
        ),
        grid=(t // tm,),
        in_specs=[
            pl.BlockSpec((tm, LANE), lambda i: (i, PR_F // LANE)),
            pl.BlockSpec((1, LANE), lambda i: (0, 0)),
            pl.BlockSpec((tm, tm), lambda i: (0, 0)),
        ],
        out_specs=(
            pl.BlockSpec((tm, FOX_H), lambda i: (i, 0)),
            pl.BlockSpec((tm, FOX_H), lambda i: (i, 0)),
            pl.BlockSpec((FOX_H, tm), lambda i: (0, i)),
        ),
        scratch_shapes=[pltpu.VMEM((1, LANE), F32)],
        compiler_params=_cparams(("arbitrary",)),
        name="logf_cumsum",
    )(pr, bf_row, tri)


def _logf_only_kernel(f_ref, bf_ref, lf_ref):
    lf_ref[...] = _log_sigmoid(f_ref[...] + bf_ref[...])[:, :FOX_H]


def _logf_only(pr, bf_row):
    t = pr.shape[0]
    return pl.pallas_call(
        _logf_only_kernel,
        out_shape=jax.ShapeDtypeStruct((t, FOX_H), F32),
        grid=(1,),
        in_specs=[
            pl.BlockSpec((t, LANE), lambda i: (0, PR_F // LANE)),
            pl.BlockSpec((1, LANE), lambda i: (0, 0)),
        ],
        out_specs=pl.BlockSpec((t, FOX_H), lambda i: (0, 0)),
        name="logf",
    )(pr, bf_row)


def _fox_prompt_kernel(q_ref, k_ref, v_ref, cq_ref, ck_ref, gain_ref, o_ref, m_ref, l_ref, acc_ref, cqc_ref, *, tq):
    qi = pl.program_id(0)
    ki = pl.program_id(1)

    @pl.when(ki == 0)
    def _():
        m_ref[...] = jnp.full_like(m_ref, -jnp.inf)
        l_ref[...] = jnp.zeros_like(l_ref)
        acc_ref[...] = jnp.zeros_like(acc_ref)
        cq = cq_ref[...]
        for h in range(FOX_H):
            cqc_ref[h] = jnp.broadcast_to(cq[:, h:h + 1], (tq, LANE))

    def step(diagonal):
        tk = k_ref.shape[0]
        if diagonal:
            keep = (lax.broadcasted_iota(jnp.int32, (tq, tq), 0) >= lax.broadcasted_iota(jnp.int32, (tq, tq), 1))
        ones = jnp.ones((tk, LANE), BF16)

        def head(h, carry):
            sl = pl.ds(pl.multiple_of(h * FOX_HD, FOX_HD), FOX_HD)
            s = _dot_nt(q_ref[:, sl], k_ref[:, sl]) - ck_ref[pl.ds(h, 1), :]
            if diagonal:
                s = jnp.where(keep, s, -jnp.inf)
            cq = cqc_ref[h]
            m_prev = m_ref[h]
            m_new = jnp.maximum(m_prev, jnp.max(s, axis=-1, keepdims=True) + cq)
            shift = m_new - cq
            p = jnp.exp2(s - jnp.concatenate([shift] * (tk // LANE), axis=1)).astype(BF16)
            alpha = jnp.exp2(m_prev - m_new)
            pv = _dot(p, jnp.concatenate([v_ref[:, sl], ones], axis=1))
            l_ref[h] = alpha * l_ref[h] + pv[:, FOX_HD:]
            acc_ref[:, sl] = alpha * acc_ref[:, sl] + pv[:, :FOX_HD]
            m_ref[h] = m_new
            return carry

        lax.fori_loop(0, FOX_H, head, 0, unroll=HEAD_UNROLL)

    @pl.when(ki < qi)
    def _():
        step(False)

    @pl.when(ki == qi)
    def _():
        step(True)
        for h in range(FOX_H):
            sl = slice(h * FOX_HD, (h + 1) * FOX_HD)
            o = acc_ref[:, sl] / l_ref[h]
            o_ref[:, sl] = _rms(o, gain_ref[:, sl])


def _fox_prompt(qkvb, c, ct, gain_row, tq):
    t = qkvb.shape[0]
    n = t // tq
    stat = pltpu.VMEM((FOX_H, tq, LANE), F32)
    return pl.pallas_call(
        functools.partial(_fox_prompt_kernel, tq=tq),
        out_shape=jax.ShapeDtypeStruct((t, FOX_W), F32),
        grid=(n, n),
        in_specs=[
            pl.BlockSpec((tq, FOX_W), lambda i, j: (i, 2)),
            pl.BlockSpec((tq, FOX_W), lambda i, j: (jnp.minimum(i, j), 0)),
            pl.BlockSpec((tq, FOX_W), lambda i, j: (jnp.minimum(i, j), 1)),
            pl.BlockSpec((tq, FOX_H), lambda i, j: (i, 0)),
            pl.BlockSpec((FOX_H, tq), lambda i, j: (0, jnp.minimum(i, j))),
            pl.BlockSpec((1, FOX_W), lambda i, j: (0, 0)),
        ],
        out_specs=pl.BlockSpec((tq, FOX_W), lambda i, j: (i, 0)),
        scratch_shapes=[stat, stat, pltpu.VMEM((tq, FOX_W), F32), stat],
        compiler_params=_cparams(("parallel", "arbitrary")),
        name="fox_prompt",
    )(qkvb, qkvb, qkvb, c, ct, gain_row)


def _decode_pages(first, last, q_ref, kn_ref, vn_ref, fn_ref, kc_refs, vc_refs, fc_refs, u2_ref, gain_ref, o_ref,
                  m_ref, l_ref, acc_ref, carry_ref):
    n_flat = PAGE_SIZE * FOX_H
    q = q_ref[0].astype(F32)

    @pl.when(first)
    def _():
        m_ref[...] = jnp.sum(q * kn_ref[0], axis=-1, keepdims=True)
        l_ref[...] = jnp.ones_like(l_ref)
        acc_ref[...] = vn_ref[0]
        carry_ref[...] = fn_ref[0]

    qb = q.astype(BF16)
    own = ((lax.broadcasted_iota(jnp.int32, (FOX_H, n_flat), 1) % FOX_H)
           == lax.broadcasted_iota(jnp.int32, (FOX_H, n_flat), 0))
    carry = carry_ref[...]
    f_pages = [fc_ref[0] for fc_ref in fc_refs]
    halves = []
    for f_page in f_pages:
        hi = f_page.astype(BF16).astype(F32)
        halves += [hi, f_page - hi]
    suffix = _dot(jnp.concatenate(halves, axis=0).astype(BF16), u2_ref[...])
    scores = []
    for i, (kc_ref, f_page) in enumerate(zip(kc_refs, f_pages)):
        kf = kc_ref[0].reshape(n_flat, FOX_HD).astype(BF16)
        bias = carry + suffix[2 * i * FOX_H:(2 * i + 1) * FOX_H] + suffix[(2 * i + 1) * FOX_H:(2 * i + 2) * FOX_H]
        scores.append(jnp.where(own, _dot_nt(qb, kf) + bias * LOG2E, -jnp.inf))
        carry = carry + jnp.sum(f_page, axis=-1, keepdims=True)
    carry_ref[...] = carry

    m_prev = m_ref[...]
    m_new = m_prev
    for s in scores:
        m_new = jnp.maximum(m_new, jnp.max(s, axis=-1, keepdims=True))
    alpha = jnp.exp2(m_prev - m_new)
    l_new = alpha * l_ref[...]
    acc = alpha * acc_ref[...]
    for s, vc_ref in zip(scores, vc_refs):
        p = jnp.exp2(s - m_new)
        l_new = l_new + jnp.sum(p, axis=-1, keepdims=True)
        acc = acc + _dot(p.astype(BF16), vc_ref[0].reshape(n_flat, FOX_HD).astype(BF16))
    l_ref[...] = l_new
    acc_ref[...] = acc
    m_ref[...] = m_new

    @pl.when(last)
    def _():
        o_ref[0] = _rms(acc / l_new, gain_ref[...])


def _ffn_decode_kernel(pt_ref, x_ref, g_ref, w1_ref, w3_ref, w2_ref, q_ref, kn_ref, vn_ref, fn_ref, *refs,
                       group, steps_per_seq, seqs_per_tile):
    del pt_ref
    kc_refs, vc_refs, fc_refs = refs[:group], refs[group:2 * group], refs[2 * group:3 * group]
    u2_ref, gain_ref, o_ref, oa_ref, hn_ref, m_ref, l_ref, acc_ref, carry_ref = refs[3 * group:]
    j = pl.program_id(1)
    _ffn_kernel(x_ref, g_ref, w1_ref, w3_ref, w2_ref, o_ref, hn_ref)

    @pl.when(j < seqs_per_tile * steps_per_seq)
    def _():
        step = j % steps_per_seq
        _decode_pages(step == 0, step == steps_per_seq - 1, q_ref, kn_ref, vn_ref, fn_ref, kc_refs, vc_refs, fc_refs,
                      u2_ref, gain_ref, oa_ref, m_ref, l_ref, acc_ref, carry_ref)


def _ffn_decode(x, g, w1, w3, w2, tm, tf, page_table, q, k_new, v_new, f_new, cache_k, cache_v, cache_ft, gain, layer):
    t, d = x.shape
    f = w1.shape[1]
    b = q.shape[0]
    n_pages = page_table.shape[1]
    group = max(gg for gg in (8, 4, 2, 1) if n_pages % gg == 0)
    steps_per_seq = n_pages // group
    n_i, n_j = t // tm, f // tf
    seqs_per_tile = b // n_i
    attn_steps = seqs_per_tile * steps_per_seq
    assert b % n_i == 0 and attn_steps <= n_j
    last = n_pages - 1
    n_flat = PAGE_SIZE * FOX_H
    u2 = (jnp.arange(PAGE_SIZE)[:, None] > (jnp.arange(n_flat) // FOX_H)[None, :]).astype(BF16)

    def seq_of(i, j):
        return i * seqs_per_tile + jnp.minimum(j, attn_steps - 1) // steps_per_seq

    def page_of(i, j, pt, g):
        return pt[seq_of(i, j), last - ((jnp.minimum(j, attn_steps - 1) % steps_per_seq) * group + g)]

    tok = pl.BlockSpec((1, FOX_H, FOX_HD), lambda i, j, pt: (seq_of(i, j), 0, 0))

    def page_kv(g):
        return pl.BlockSpec((None, 1, PAGE_SIZE, FOX_H, FOX_HD), lambda i, j, pt: (layer, page_of(i, j, pt, g), 0, 0, 0))

    def page_f(g):
        return pl.BlockSpec((None, 1, FOX_H, PAGE_SIZE), lambda i, j, pt: (layer, page_of(i, j, pt, g), 0, 0))

    stat = pltpu.VMEM((FOX_H, 1), F32)
    grid_spec = pltpu.PrefetchScalarGridSpec(
        num_scalar_prefetch=1,
        grid=(n_i, n_j),
        in_specs=[
            pl.BlockSpec((tm, d), lambda i, j, pt: (i, 0)),
            pl.BlockSpec((1, d), lambda i, j, pt: (0, 0)),
            pl.BlockSpec((d, tf), lambda i, j, pt: (0, j)),
            pl.BlockSpec((d, tf), lambda i, j, pt: (0, j)),
            pl.BlockSpec((tf, d), lambda i, j, pt: (j, 0)),
            tok, tok, tok,
            pl.BlockSpec((1, FOX_H, 1), lambda i, j, pt: (seq_of(i, j), 0, 0)),
            *[page_kv(g) for g in range(group)], *[page_kv(g) for g in range(group)],
            *[page_f(g) for g in range(group)],
            pl.BlockSpec((PAGE_SIZE, n_flat), lambda i, j, pt: (0, 0)),
            pl.BlockSpec((FOX_H, FOX_HD), lambda i, j, pt: (0, 0)),
        ],
        out_specs=(pl.BlockSpec((tm, d), lambda i, j, pt: (i, 0)), tok),
        scratch_shapes=[pltpu.VMEM((tm, d), BF16), stat, stat, pltpu.VMEM((FOX_H, FOX_HD), F32), stat],
    )
    return pl.pallas_call(
        functools.partial(_ffn_decode_kernel, group=group, steps_per_seq=steps_per_seq, seqs_per_tile=seqs_per_tile),
        out_shape=(jax.ShapeDtypeStruct((t, d), F32), jax.ShapeDtypeStruct((b, FOX_H, FOX_HD), F32)),
        grid_spec=grid_spec,
        compiler_params=_cparams(("arbitrary", "arbitrary")),
        name="ffn_decode",
    )(page_table, x, g, w1, w3, w2, q, k_new, v_new, f_new, *([cache_k] * group), *([cache_v] * group),
      *([cache_ft] * group), u2, gain)


def _head_sum(x, ind, ind_t):
    hi, lo = _split(x)
    s = _dot(hi, ind) + _dot(lo, ind)
    s_hi, s_lo = _split(s)
    return _dot(s_hi, ind_t) + _dot(s_lo, ind_t)


def _rwkv_prep_kernel(pr_ref, prev_ref, mu_ref, vec_ref, wwa_ref, wg_ref, ind_ref, indt_ref,
                      r_ref, lw_ref, k_ref, v_ref, kk_ref, b_ref, g_ref, bonus_ref, carry_ref, *, sequential):
    pr = pr_ref[...]
    if sequential:
        @pl.when(pl.program_id(0) == 0)
        def _():
            carry_ref[...] = jnp.zeros_like(carry_ref)

        rolled = pltpu.roll(pr, 1, 0)
        first = lax.broadcasted_iota(jnp.int32, pr.shape, 0) == 0
        prev = jnp.where(first, carry_ref[...], rolled)
        carry_ref[...] = pr[-1:, :]
    else:
        prev = prev_ref[...]
    xs = pr + (prev - pr) * mu_ref[...]
    r = xs[:, 0:RW_W]
    kr = xs[:, RW_W:2 * RW_W]
    vr = xs[:, 2 * RW_W:3 * RW_W]
    wa = xs[:, PR_WA:PR_WA + LANE]
    gd = xs[:, PR_G:PR_W]
    w0, a0, k_k, k_a, r_k = (vec_ref[i:i + 1, :] for i in range(5))

    is_decay = lax.broadcasted_iota(jnp.int32, wa.shape, 1) < R_DECAY
    wa_act = jnp.where(is_decay, jnp.tanh(wa), wa).astype(BF16)
    lora = _dot(wa_act, wwa_ref[...])
    x_w = w0 + lora[:, :RW_W]
    w_log = -(jnp.maximum(-x_w, 0.0) + jnp.log1p(jnp.exp(-jnp.abs(x_w)))) - 0.5
    lw_ref[...] = -jnp.exp(w_log)
    a = jax.nn.sigmoid(a0 + lora[:, RW_W:])
    g_ref[...] = _dot(jax.nn.sigmoid(gd).astype(BF16), wg_ref[...])

    kk = kr * k_k
    ss = _head_sum(kk * kk, ind_ref[...], indt_ref[...])
    kk = kk / jnp.maximum(jnp.sqrt(ss), 1e-12)
    k2 = kr * (1.0 + (a - 1.0) * k_a)
    bonus_ref[...] = _head_sum(r * k2 * r_k, ind_ref[...], indt_ref[...]) * vr
    r_ref[...] = r
    k_ref[...] = k2
    v_ref[...] = vr
    kk_ref[...] = kk
    b_ref[...] = kk * a


def _rwkv_prep(pr, prev, mu_row, vecs, w_wa, w_g, ind, ind_t, tm, sequential):
    t = pr.shape[0]
    row = lambda i: (i, 0)
    const = lambda i: (0, 0)
    out = jax.ShapeDtypeStruct((t, RW_W), F32)
    return pl.pallas_call(
        functools.partial(_rwkv_prep_kernel, sequential=sequential),
        out_shape=(out,) * 8,
        grid=(t // tm,),
        in_specs=[
            pl.BlockSpec((tm, PR_W), row),
            pl.BlockSpec((prev.shape[0] if sequential else tm, PR_W), const if sequential else row),
            pl.BlockSpec((1, PR_W), const),
            pl.BlockSpec((8, RW_W), const),
            pl.BlockSpec((LANE, 2 * RW_W), const),
            pl.BlockSpec((2 * LANE, RW_W), const),
            pl.BlockSpec((RW_W, LANE), const),
            pl.BlockSpec((LANE, RW_W), const),
        ],
        out_specs=(pl.BlockSpec((tm, RW_W), row),) * 8,
        scratch_shapes=[pltpu.VMEM((1, PR_W), F32)],
        compiler_params=_cparams(("arbitrary",)),
        name="rwkv_prep",
    )(pr, prev, mu_row, vecs, w_wa, w_g, ind, ind_t)


def _wkv_chunk_kernel(r_ref, lw_ref, k_ref, v_ref, kk_ref, b_ref, s0_ref, y_ref, s_ref):
    L = WKV_L

    @pl.when(pl.program_id(0) == 0)
    def _():
        s_ref[...] = s0_ref[...]

    lw = lw_ref[...]
    tril = (lax.broadcasted_iota(jnp.int32, (L, L), 0) >= lax.broadcasted_iota(jnp.int32, (L, L), 1)).astype(BF16)
    lw_hi, lw_lo = _split(lw)
    gcum = _dot(tril, lw_hi) + _dot(tril, lw_lo)
    e_g = jnp.exp(gcum)
    e_gi = jnp.exp(-gcum)
    rt = r_ref[...] * e_g
    kt = k_ref[...] * e_gi
    bt = b_ref[...] * e_gi
    kkt = kk_ref[...] * jnp.exp(gcum - lw)
    v = v_ref[...]

    row = lax.broadcasted_iota(jnp.int32, (2 * L, 2 * L), 0)
    col = lax.broadcasted_iota(jnp.int32, (2 * L, 2 * L), 1)
    same = (row // L) == (col // L)
    strict = same & ((col % L) < (row % L))
    incl = same & ((col % L) <= (row % L))
    head0 = lax.broadcasted_iota(jnp.int32, (L, 2 * L), 1) < RW_N

    def bd(x):
        return jnp.concatenate([jnp.where(head0, x, 0.0), jnp.where(head0, 0.0, x)], axis=0).astype(BF16)

    def dup(x):
        xb = x.astype(BF16)
        return jnp.concatenate([xb, xb], axis=0)

    pairs = range(RW_H // 2)
    sls = [slice(p * 2 * RW_N, (p + 1) * 2 * RW_N) for p in pairs]
    kkr_bd = [jnp.concatenate([bd(kkt[:, sl]), bd(rt[:, sl])], axis=0) for sl in sls]
    bk_bd = [jnp.concatenate([bd(bt[:, sl]), bd(kt[:, sl])], axis=0) for sl in sls]
    bk_dup = [jnp.concatenate([dup(bt[:, sl]), dup(kt[:, sl])], axis=0) for sl in sls]
    v_bd = [bd(v[:, sl]) for sl in sls]
    sp = [s_ref[p] for p in pairs]
    a_all = [_dot_nt(kkr_bd[p], bk_dup[p]) for p in pairs]
    from_state = [_dot_nt(kkr_bd[p], sp[p].astype(BF16)) for p in pairs]
    n = [jnp.where(strict, a[:2 * L, :2 * L], 0.0) for a in a_all]
    a_bk = [jnp.where(strict, a[:2 * L, 2 * L:], 0.0).astype(BF16) for a in a_all]
    a_r = [jnp.concatenate([jnp.where(incl, a[2 * L:, :2 * L], 0.0), jnp.where(incl, a[2 * L:, 2 * L:], 0.0)],
                           axis=1).astype(BF16) for a in a_all]
    x = [-(from_state[p][:2 * L] + _dot(a_bk[p], v_bd[p])) for p in pairs]
    x = [x[p] - _mm(n[p], x[p]) for p in pairs]
    steps = 1
    while 2 * steps < L:
        n = [_mm(n[p], n[p]) for p in pairs]
        x = [x[p] + _mm(n[p], x[p]) for p in pairs]
        steps *= 2
    dv = [jnp.concatenate([x[p].astype(BF16), v_bd[p]], axis=0) for p in pairs]
    for p in pairs:
        y_bd = from_state[p][2 * L:] + _dot(a_r[p], dv[p])
        y_ref[:, sls[p]] = y_bd[:L] + y_bd[L:]
        s_ref[p] = (sp[p] + _dot_tn(dv[p], bk_bd[p])) * e_g[L - 1:L, sls[p]]


def _mm(a, b):
    return _dot(a.astype(BF16), b.astype(BF16))


def _wkv_chunked(r, lw, k, v, kk, b, s0_pairs):
    t = r.shape[0]
    row = lambda i: (i, 0)
    n_pairs = RW_H // 2
    spec = pl.BlockSpec((WKV_L, RW_W), row)
    sspec = pl.BlockSpec((n_pairs, LANE, LANE), lambda i: (0, 0, 0))
    return pl.pallas_call(
        _wkv_chunk_kernel,
        out_shape=(jax.ShapeDtypeStruct((t, RW_W), F32), jax.ShapeDtypeStruct((n_pairs, LANE, LANE), F32)),
        grid=(t // WKV_L,),
        in_specs=[spec] * 6 + [sspec],
        out_specs=(spec, sspec),
        compiler_params=_cparams(("arbitrary",)),
        name="wkv_chunk",
    )(r, lw, k, v, kk, b, s0_pairs)


def _wkv_step_kernel(s_ref, r_ref, lw_ref, k_ref, v_ref, kk_ref, b_ref, y_ref, so_ref):
    s = s_ref[0]
    s_kk = jnp.sum(s * kk_ref[0], axis=-1, keepdims=True)
    s = s * jnp.exp(lw_ref[0]) - s_kk * b_ref[0] + v_ref[0] * k_ref[0]
    so_ref[0] = s
    y_ref[0] = jnp.sum(s * r_ref[0], axis=-1, keepdims=True)


def _wkv_step(state, r, lw, k, v, kk, b):
    bsz = state.shape[0]
    keyed = lambda x: x.reshape(bsz, RW_H, 1, RW_N)
    kspec = pl.BlockSpec((1, RW_H, 1, RW_N), lambda i: (i, 0, 0, 0))
    vspec = pl.BlockSpec((1, RW_H, RW_N, 1), lambda i: (i, 0, 0, 0))
    sspec = pl.BlockSpec((1, RW_H, RW_N, RW_N), lambda i: (i, 0, 0, 0))
    y, s_new = pl.pallas_call(
        _wkv_step_kernel,
        out_shape=(jax.ShapeDtypeStruct((bsz, RW_H, RW_N, 1), F32), jax.ShapeDtypeStruct(state.shape, F32)),
        grid=(bsz,),
        in_specs=[sspec, kspec, kspec, kspec, vspec, kspec, kspec],
        out_specs=(vspec, sspec),
        compiler_params=_cparams(("parallel",)),
        name="wkv_step",
    )(state, keyed(r), keyed(lw), keyed(k), v.reshape(bsz, RW_H, RW_N, 1), keyed(kk), keyed(b))
    return y.reshape(bsz, RW_W), s_new


def _mix_out_kernel(x_ref, of_ref, y_ref, bonus_ref, g_ref, lnw_ref, lnb_ref, ind_ref, indt_ref, wo_ref, o_ref):
    y = y_ref[...]
    ind, ind_t = ind_ref[...], indt_ref[...]
    mean = _head_sum(y, ind, ind_t) * (1.0 / RW_N)
    yc = y - mean
    var = _head_sum(yc * yc, ind, ind_t) * (1.0 / RW_N)
    yn = yc * lax.rsqrt(var + LNX_EPS) * lnw_ref[...] + lnb_ref[...]
    o_rw = ((yn + bonus_ref[...]) * g_ref[...]).astype(BF16)
    o_ref[...] = (x_ref[...] + _dot(of_ref[...].astype(BF16), wo_ref[:FOX_W, :]) + _dot(o_rw, wo_ref[FOX_W:, :]))


def _mix_out(x, o_fox, y, bonus, g, lnw_row, lnb_row, ind, ind_t, w_out, tm):
    t, d = x.shape
    row = lambda i: (i, 0)
    const = lambda i: (0, 0)
    half = pl.BlockSpec((tm, RW_W), row)
    return pl.pallas_call(
        _mix_out_kernel,
        out_shape=jax.ShapeDtypeStruct((t, d), F32),
        grid=(t // tm,),
        in_specs=[
            pl.BlockSpec((tm, d), row), pl.BlockSpec((tm, FOX_W), row), half, half, half,
            pl.BlockSpec((1, RW_W), const), pl.BlockSpec((1, RW_W), const),
            pl.BlockSpec((RW_W, LANE), const), pl.BlockSpec((LANE, RW_W), const),
            pl.BlockSpec((d, d), const),
        ],
        out_specs=pl.BlockSpec((tm, d), row),
        compiler_params=_cparams(("parallel",)),
        name="mix_out",
    )(x, o_fox, y, bonus, g, lnw_row, lnb_row, ind, ind_t, w_out)


def _final_norm_kernel(x_ref, g_ref, o_ref):
    o_ref[...] = _rms(x_ref[...], g_ref[...])


def _final_norm(x, g, tm):
    t, d = x.shape
    return pl.pallas_call(
        _final_norm_kernel,
        out_shape=jax.ShapeDtypeStruct((t, d), F32),
        grid=(t // tm,),
        in_specs=[pl.BlockSpec((tm, d), lambda i: (i, 0)), pl.BlockSpec((1, d), lambda i: (0, 0))],
        out_specs=pl.BlockSpec((tm, d), lambda i: (i, 0)),
        compiler_params=_cparams(("parallel",)),
        name="final_norm",
    )(x, g)


def _pack_cols(a):
    lead = a.shape[:-1]
    z = lambda n: jnp.zeros(lead + (n,), a.dtype)
    return jnp.concatenate([a[..., :PR_WA + LANE], z(LANE), a[..., PR_WA + LANE:], z(2 * LANE - R_G)], axis=-1)


def _unpack_cols(a):
    return jnp.concatenate([a[..., :PR_WA + LANE], a[..., PR_G:PR_G + R_G]], axis=-1)


def _layer_weights(l, norm_ffa, ffa_w1, ffa_w3, ffa_w2, norm_mix, w_in, b_f, fox_gain, mu_shift, w0, w_up,
                   a0, a_up, g_up, k_k, k_a, r_k, lnx_w, lnx_b, w_out, norm_ffb, ffb_w1, ffb_w3, ffb_w2):
    d = D_MODEL
    w_rw = _pack_cols(w_in[l][:, C_FOX_IN:])
    w_rw = w_rw.at[:, PR_F:PR_F + FOX_H].set(w_in[l][:, 3 * FOX_W:C_FOX_IN])
    zeros = jnp.zeros((R_DECAY, RW_W), F32)
    w_wa = jnp.concatenate([jnp.concatenate([w_up[l], zeros], axis=1), jnp.concatenate([zeros, a_up[l]], axis=1)], axis=0)
    w_g = jnp.concatenate([g_up[l], jnp.zeros((2 * LANE - R_G, RW_W), F32)], axis=0)
    vecs = jnp.stack([w0[l], a0[l], k_k[l], k_a[l], r_k[l].reshape(RW_W)] + [jnp.zeros((RW_W,), F32)] * 3)
    return dict(
        norm_ffa=norm_ffa[l].reshape(1, d), ffa_w1=ffa_w1[l].astype(BF16), ffa_w3=ffa_w3[l].astype(BF16),
        ffa_w2=ffa_w2[l].astype(BF16),
        norm_mix=norm_mix[l].reshape(1, d), w_rw=w_rw.astype(BF16),
        w_kvq=jnp.concatenate([w_in[l][:, FOX_W:3 * FOX_W], w_in[l][:, :FOX_W]], axis=1).astype(BF16),
        bf_row=jnp.pad(b_f[l], (0, LANE - FOX_H)).reshape(1, LANE),
        gain_row=fox_gain[l].reshape(1, FOX_W), mu_row=_pack_cols(mu_shift[l]).reshape(1, PR_W),
        vecs=vecs, w_wa=w_wa.astype(BF16), w_g=w_g.astype(BF16),
        lnw_row=lnx_w[l].reshape(1, RW_W), lnb_row=lnx_b[l].reshape(1, RW_W), w_out=w_out[l].astype(BF16),
        norm_ffb=norm_ffb[l].reshape(1, d), ffb_w1=ffb_w1[l].astype(BF16), ffb_w3=ffb_w3[l].astype(BF16),
        ffb_w2=ffb_w2[l].astype(BF16),
    )


def _head_indicators():
    lane_head = jnp.arange(RW_W) // RW_N
    ind = (lane_head[:, None] == jnp.arange(LANE)[None, :]).astype(BF16)
    return ind, ind.T


def _pairs_from_state(s):
    s = s.reshape(RW_H // 2, 2, RW_N, RW_N)
    z = jnp.zeros_like(s[:, 0])
    return jnp.concatenate([jnp.concatenate([s[:, 0], z], axis=2), jnp.concatenate([z, s[:, 1]], axis=2)], axis=1)


def _state_from_pairs(sp):
    return jnp.stack([sp[:, :RW_N, :RW_N], sp[:, RW_N:, RW_N:]], axis=1).reshape(RW_H, RW_N, RW_N)


def _tile(t, pref):
    return pref if t % pref == 0 else t


def _layer(xp, xs, lw_, ind, ind_t, layer, cache_k, cache_v, cache_ft, page_table, state, shift_prev):
    t, bsz = xp.shape[0], xs.shape[0]
    tm = _tile(t, 512)
    ffa = (lw_["norm_ffa"], lw_["ffa_w1"], lw_["ffa_w3"], lw_["ffa_w2"])
    ffb = (lw_["norm_ffb"], lw_["ffb_w1"], lw_["ffb_w3"], lw_["ffb_w2"])

    xs = _ffn(xs, *ffa, bsz, 1408)
    k_s, v_s, qkvb_s = _proj_qkv(xs, lw_["norm_mix"], lw_["w_kvq"], bsz)
    pr_s = _norm_matmul(xs, lw_["norm_mix"], lw_["w_rw"], bsz, PR_W // 4)
    logf_s = _logf_only(pr_s, lw_["bf_row"])
    q_s = qkvb_s[:, 2 * FOX_W:].reshape(bsz, FOX_H, FOX_HD)

    xp, o_fox_s = _ffn_decode(xp, *ffa, tm, 256, page_table, q_s, k_s, v_s, logf_s.reshape(bsz, FOX_H, 1),
                              cache_k, cache_v, cache_ft, lw_["gain_row"].reshape(FOX_H, FOX_HD), layer)

    k_p, v_p, qkvb = _proj_qkv(xp, lw_["norm_mix"], lw_["w_kvq"], tm)
    pr = _norm_matmul(xp, lw_["norm_mix"], lw_["w_rw"], tm, PR_W // 4)
    logf, c, ct = _logf(pr, lw_["bf_row"], _tile(t, 256))
    o_fox = _fox_prompt(qkvb, c, ct, lw_["gain_row"], tm)
    r, lw, k, v, kk, b, g, bonus = _rwkv_prep(pr, lw_["mu_row"], lw_["mu_row"], lw_["vecs"], lw_["w_wa"], lw_["w_g"],
                                              ind, ind_t, _tile(t, 256), True)
    y, s_pairs = _wkv_chunked(r, lw, k, v, kk, b, jnp.zeros((RW_H // 2, LANE, LANE), F32))
    xp = _mix_out(xp, o_fox, y, bonus, g, lw_["lnw_row"], lw_["lnb_row"], ind, ind_t, lw_["w_out"], _tile(t, 256))
    xp = _ffn(xp, *ffb, tm, 512)
    out_p = (k_p[None], v_p[None], logf.reshape(1, t, FOX_H), _state_from_pairs(s_pairs)[None],
             _unpack_cols(pr[t - 1:t, :]))

    r, lw, k, v, kk, b, g, bonus = _rwkv_prep(pr_s, _pack_cols(shift_prev), lw_["mu_row"], lw_["vecs"], lw_["w_wa"],
                                              lw_["w_g"], ind, ind_t, bsz, False)
    y, s_new = _wkv_step(state, r, lw, k, v, kk, b)
    xs = _mix_out(xs, o_fox_s.reshape(bsz, FOX_W), y, bonus, g, lw_["lnw_row"], lw_["lnb_row"], ind, ind_t,
                  lw_["w_out"], bsz)
    xs = _ffn(xs, *ffb, bsz, 1408)
    out_s = (k_s.reshape(bsz, 1, FOX_H, FOX_HD), v_s.reshape(bsz, 1, FOX_H, FOX_HD), logf_s.reshape(bsz, 1, FOX_H),
             s_new, _unpack_cols(pr_s))
    return xp, xs, out_p, out_s


def kernel(x_prompt, x_sample, cache_k, cache_v, cache_logf, state_rwkv, state_shift, page_table, norm_ffa, ffa_w1, ffa_w3, ffa_w2, norm_mix, w_in, b_f, fox_gain, mu_shift, w0, w_up, a0, a_up, g_up, k_k, k_a, r_k, lnx_w, lnx_b, w_out, norm_ffb, ffb_w1, ffb_w3, ffb_w2, norm_final):
    depth = norm_ffa.shape[0]
    b_p, seq, d = x_prompt.shape
    b_s = x_sample.shape[0]
    assert b_p == 1 and x_sample.shape[1] == 1 and seq % WKV_L == 0
    cache_ft = jnp.swapaxes(cache_logf, 2, 3)
    ind, ind_t = _head_indicators()
    xp = x_prompt.reshape(seq, d)
    xs = x_sample.reshape(b_s, d)
    outs_p, outs_s = [], []
    for l in range(depth):
        lw_ = _layer_weights(l, norm_ffa, ffa_w1, ffa_w3, ffa_w2, norm_mix, w_in, b_f, fox_gain, mu_shift, w0, w_up,
                             a0, a_up, g_up, k_k, k_a, r_k, lnx_w, lnx_b, w_out, norm_ffb, ffb_w1, ffb_w3, ffb_w2)
        xp, xs, out_p, out_s = _layer(xp, xs, lw_, ind, ind_t, l, cache_k, cache_v, cache_ft, page_table,
                                      state_rwkv[l], state_shift[l])
        outs_p.append(out_p)
        outs_s.append(out_s)
    g_final = norm_final.reshape(1, d)
    y_prompt = _final_norm(xp, g_final, _tile(seq, 512)).reshape(b_p, seq, d)
    y_sample = _final_norm(xs, g_final, b_s).reshape(b_s, 1, d)
    stack = lambda outs, i: jnp.stack([o[i] for o in outs])
    return (y_prompt, y_sample,
            stack(outs_p, 0), stack(outs_p, 1), stack(outs_p, 2), stack(outs_p, 3), stack(outs_p, 4),
            stack(outs_s, 0), stack(outs_s, 1), stack(outs_s, 2), stack(outs_s, 3), stack(outs_s, 4))
```

```python
import functools

import jax
import jax.numpy as jnp
from jax import lax
from jax.experimental import pallas as pl
from jax.experimental.pallas import tpu as pltpu

F32 = jnp.float32
BF16 = jnp.bfloat16

D_MODEL = 2048
FOX_HD = 128
FOX_W = D_MODEL // 2
FOX_H = FOX_W // FOX_HD
RW_N = 64
RW_W = D_MODEL - FOX_W
RW_H = RW_W // RW_N
R_DECAY = 64
R_A = 64
R_G = 160
PAGE_SIZE = 128
C_FOX_IN = 3 * FOX_W + FOX_H
C_SHIFT = 3 * RW_W + R_DECAY + R_A + R_G
RMS_EPS = 1e-6
LNX_EPS = 64e-5
FFN_RES = 0.5

LOG2E = 1.4426950408889634
Q_SCALE = FOX_HD ** -0.5 * LOG2E
LANE = 128
PR_WA = 3 * RW_W
PR_F = PR_WA + LANE
PR_G = PR_F + LANE
PR_W = PR_G + 2 * LANE
PR_TN = 512
WKV_L = 64
HEAD_UNROLL = 4
MAX_PAGES_PER_STEP = 16
VMEM_LIMIT = 56 * 1024 * 1024


def _cparams(sem):
    return pltpu.CompilerParams(dimension_semantics=sem, vmem_limit_bytes=VMEM_LIMIT)


def _dot(a, b):
    return jnp.dot(a, b, preferred_element_type=F32)


def _dot_nt(a, b):
    return lax.dot_general(a, b, (((1,), (1,)), ((), ())), preferred_element_type=F32)


def _dot_tn(a, b):
    return lax.dot_general(a, b, (((0,), (0,)), ((), ())), preferred_element_type=F32)


def _split(x):
    hi = x.astype(BF16)
    lo = (x - hi.astype(F32)).astype(BF16)
    return hi, lo


def _rms(x, g):
    return x * lax.rsqrt(jnp.mean(x * x, axis=-1, keepdims=True) + RMS_EPS) * g


def _ffn_kernel(x_ref, g_ref, w1_ref, w3_ref, w2_ref, o_ref, hn_ref):
    @pl.when(pl.program_id(1) == 0)
    def _():
        x = x_ref[...]
        hn_ref[...] = _rms(x, g_ref[...]).astype(BF16)
        o_ref[...] = x

    hn = hn_ref[...]
    h1 = _dot(hn, w1_ref[...])
    h3 = _dot(hn, w3_ref[...])
    a = (h1 * jax.nn.sigmoid(h1) * h3 * FFN_RES).astype(BF16)
    o_ref[...] += _dot(a, w2_ref[...])


def _ffn(x, g, w1, w3, w2, tm, tf):
    t, d = x.shape
    f = w1.shape[1]
    return pl.pallas_call(
        _ffn_kernel,
        out_shape=jax.ShapeDtypeStruct((t, d), F32),
        grid=(t // tm, f // tf),
        in_specs=[
            pl.BlockSpec((tm, d), lambda i, j: (i, 0)),
            pl.BlockSpec((1, d), lambda i, j: (0, 0)),
            pl.BlockSpec((d, tf), lambda i, j: (0, j)),
            pl.BlockSpec((d, tf), lambda i, j: (0, j)),
            pl.BlockSpec((tf, d), lambda i, j: (j, 0)),
        ],
        out_specs=pl.BlockSpec((tm, d), lambda i, j: (i, 0)),
        scratch_shapes=[pltpu.VMEM((tm, d), BF16)],
        compiler_params=_cparams(("parallel", "arbitrary")),
        name="ffn",
    )(x, g, w1, w3, w2)


def _norm_matmul_kernel(x_ref, g_ref, w_ref, o_ref, hn_ref):
    @pl.when(pl.program_id(1) == 0)
    def _():
        hn_ref[...] = _rms(x_ref[...], g_ref[...]).astype(BF16)

    o_ref[...] = _dot(hn_ref[...], w_ref[...])


def _norm_matmul(x, g, w, tm, tn, col0, n):
    t, d = x.shape
    first = col0 // tn
    assert col0 % tn == 0 and n % tn == 0
    return pl.pallas_call(
        _norm_matmul_kernel,
        out_shape=jax.ShapeDtypeStruct((t, n), F32),
        grid=(t // tm, n // tn),
        in_specs=[
            pl.BlockSpec((tm, d), lambda i, j: (i, 0)),
            pl.BlockSpec((1, d), lambda i, j: (0, 0)),
            pl.BlockSpec((d, tn), lambda i, j: (0, first + j)),
        ],
        out_specs=pl.BlockSpec((tm, tn), lambda i, j: (i, j)),
        scratch_shapes=[pltpu.VMEM((tm, d), BF16)],
        compiler_params=_cparams(("parallel", "arbitrary")),
        name="norm_matmul",
    )(x, g, w)


def _proj_qkv_kernel(x_ref, g_ref, w_ref, k_ref, v_ref, qkvb_ref, hn_ref):
    j = pl.program_id(1)

    @pl.when(j == 0)
    def _():
        hn_ref[...] = _rms(x_ref[...], g_ref[...]).astype(BF16)

    res = _dot(hn_ref[...], w_ref[...])

    def emit(head_ref):
        for h in range(FOX_H):
            head_ref[:, h, :] = res[:, h * FOX_HD:(h + 1) * FOX_HD]
        qkvb_ref[...] = res.astype(BF16)

    pl.when(j == 0)(lambda: emit(k_ref))
    pl.when(j == 1)(lambda: emit(v_ref))

    @pl.when(j == 2)
    def _():
        qkvb_ref[...] = (res * Q_SCALE).astype(BF16)


def _proj_qkv(x, g, w_kvq, tm):
    t, d = x.shape
    heads = jax.ShapeDtypeStruct((t, FOX_H, FOX_HD), F32)
    hspec = pl.BlockSpec((tm, FOX_H, FOX_HD), lambda i, j: (i, 0, 0))
    return pl.pallas_call(
        _proj_qkv_kernel,
        out_shape=(heads, heads, jax.ShapeDtypeStruct((t, 3 * FOX_W), BF16)),
        grid=(t // tm, 3),
        in_specs=[
            pl.BlockSpec((tm, d), lambda i, j: (i, 0)),
            pl.BlockSpec((1, d), lambda i, j: (0, 0)),
            pl.BlockSpec((d, FOX_W), lambda i, j: (0, j)),
        ],
        out_specs=(hspec, hspec, pl.BlockSpec((tm, FOX_W), lambda i, j: (i, j))),
        scratch_shapes=[pltpu.VMEM((tm, d), BF16)],
        compiler_params=_cparams(("parallel", "arbitrary")),
        name="proj_qkv",
    )(x, g, w_kvq)


def _log_sigmoid(z):
    return jnp.minimum(z, 0.0) - jnp.log1p(jnp.exp(-jnp.abs(z)))


def _logf_kernel(f_ref, bf_ref, tri_ref, lf_ref, c_ref, ct_ref, carry_ref):
    @pl.when(pl.program_id(0) == 0)
    def _():
        carry_ref[...] = jnp.zeros_like(carry_ref)

    lf = _log_sigmoid(f_ref[...] + bf_ref[...])
    hi, lo = _split(lf)
    c = _dot(tri_ref[...], hi) + _dot(tri_ref[...], lo) + carry_ref[...]
    carry_ref[...] = c[-1:, :]
    lf_ref[...] = lf[:, :FOX_H]
    c2 = c * LOG2E
    c_ref[...] = c2[:, :FOX_H]
    ct_ref[...] = c2.T[:FOX_H, :]


def _logf(pr, bf_row, tm):
    t = pr.shape[0]
    tri = (lax.broadcasted_iota(jnp.int32, (tm, tm), 0) >= lax.broadcasted_iota(jnp.int32, (tm, tm), 1)).astype(BF16)
    return pl.pallas_call(
        _logf_kernel,
        out_shape=(
            jax.ShapeDtypeStruct((t, FOX_H), F32),
            jax.ShapeDtypeStruct((t, FOX_H), F32),
            jax.ShapeDtypeStruct((FOX_H, t), F32),
        ),
        grid=(t // tm,),
        in_specs=[
            pl.BlockSpec((tm, LANE), lambda i: (i, PR_F // LANE)),
            pl.BlockSpec((1, LANE), lambda i: (0, 0)),
            pl.BlockSpec((tm, tm), lambda i: (0, 0)),
        ],
        out_specs=(
            pl.BlockSpec((tm, FOX_H), lambda i: (i, 0)),
            pl.BlockSpec((tm, FOX_H), lambda i: (i, 0)),
            pl.BlockSpec((FOX_H, tm), lambda i: (0, i)),
        ),
        scratch_shapes=[pltpu.VMEM((1, LANE), F32)],
        compiler_params=_cparams(("arbitrary",)),
        name="logf_cumsum",
    )(pr, bf_row, tri)


def _logf_only_kernel(f_ref, bf_ref, lf_ref):
    lf_ref[...] = _log_sigmoid(f_ref[...] + bf_ref[...])[:, :FOX_H]


def _logf_only(pr, bf_row):
    t = pr.shape[0]
    return pl.pallas_call(
        _logf_only_kernel,
        out_shape=jax.ShapeDtypeStruct((t, FOX_H), F32),
        grid=(1,),
        in_specs=[
            pl.BlockSpec((t, LANE), lambda i: (0, PR_F // LANE)),
            pl.BlockSpec((1, LANE), lambda i: (0, 0)),
        ],
        out_specs=pl.BlockSpec((t, FOX_H), lambda i: (0, 0)),
        name="logf",
    )(pr, bf_row)


def _fox_prompt_kernel(q_ref, k_ref, v_ref, cq_ref, ck_ref, gain_ref, o_ref, m_ref, l_ref, acc_ref, cqc_ref, *, tq):
    qi = pl.program_id(0)
    ki = pl.program_id(1)

    @pl.when(ki == 0)
    def _():
        m_ref[...] = jnp.full_like(m_ref, -jnp.inf)
        l_ref[...] = jnp.zeros_like(l_ref)
        acc_ref[...] = jnp.zeros_like(acc_ref)
        cq = cq_ref[...]
        for h in range(FOX_H):
            cqc_ref[h] = jnp.broadcast_to(cq[:, h:h + 1], (tq, LANE))

    def step(diagonal):
        tk = k_ref.shape[0]
        if diagonal:
            keep = (lax.broadcasted_iota(jnp.int32, (tq, tq), 0) >= lax.broadcasted_iota(jnp.int32, (tq, tq), 1))
        ones = jnp.ones((tk, LANE), BF16)

        def head(h, carry):
            sl = pl.ds(pl.multiple_of(h * FOX_HD, FOX_HD), FOX_HD)
            s = _dot_nt(q_ref[:, sl], k_ref[:, sl]) - ck_ref[pl.ds(h, 1), :]
            if diagonal:
                s = jnp.where(keep, s, -jnp.inf)
            cq = cqc_ref[h]
            m_prev = m_ref[h]
            m_new = jnp.maximum(m_prev, jnp.max(s, axis=-1, keepdims=True) + cq)
            shift = m_new - cq
            p = jnp.exp2(s - jnp.concatenate([shift] * (tk // LANE), axis=1)).astype(BF16)
            alpha = jnp.exp2(m_prev - m_new)
            pv = _dot(p, jnp.concatenate([v_ref[:, sl], ones], axis=1))
            l_ref[h] = alpha * l_ref[h] + pv[:, FOX_HD:]
            acc_ref[:, sl] = alpha * acc_ref[:, sl] + pv[:, :FOX_HD]
            m_ref[h] = m_new
            return carry

        lax.fori_loop(0, FOX_H, head, 0, unroll=HEAD_UNROLL)

    @pl.when(ki < qi)
    def _():
        step(False)

    @pl.when(ki == qi)
    def _():
        step(True)
        for h in range(FOX_H):
            sl = slice(h * FOX_HD, (h + 1) * FOX_HD)
            o = acc_ref[:, sl] / l_ref[h]
            o_ref[:, sl] = _rms(o, gain_ref[:, sl])


def _fox_prompt(qkvb, c, ct, gain_row, tq):
    t = qkvb.shape[0]
    n = t // tq
    stat = pltpu.VMEM((FOX_H, tq, LANE), F32)
    return pl.pallas_call(
        functools.partial(_fox_prompt_kernel, tq=tq),
        out_shape=jax.ShapeDtypeStruct((t, FOX_W), F32),
        grid=(n, n),
        in_specs=[
            pl.BlockSpec((tq, FOX_W), lambda i, j: (i, 2)),
            pl.BlockSpec((tq, FOX_W), lambda i, j: (jnp.minimum(i, j), 0)),
            pl.BlockSpec((tq, FOX_W), lambda i, j: (jnp.minimum(i, j), 1)),
            pl.BlockSpec((tq, FOX_H), lambda i, j: (i, 0)),
            pl.BlockSpec((FOX_H, tq), lambda i, j: (0, jnp.minimum(i, j))),
            pl.BlockSpec((1, FOX_W), lambda i, j: (0, 0)),
        ],
        out_specs=pl.BlockSpec((tq, FOX_W), lambda i, j: (i, 0)),
        scratch_shapes=[stat, stat, pltpu.VMEM((tq, FOX_W), F32), stat],
        compiler_params=_cparams(("parallel", "arbitrary")),
        name="fox_prompt",
    )(qkvb, qkvb, qkvb, c, ct, gain_row)


def _decode_pages(first, last, q_ref, kn_ref, vn_ref, fn_ref, kc_refs, vc_refs, fc_refs, u2_ref, gain_ref, o_ref,
                  m_ref, l_ref, acc_ref, carry_ref):
    n_flat = PAGE_SIZE * FOX_H
    q = q_ref[0].astype(F32)

    @pl.when(first)
    def _():
        m_ref[...] = jnp.sum(q * kn_ref[0], axis=-1, keepdims=True)
        l_ref[...] = jnp.ones_like(l_ref)
        acc_ref[...] = vn_ref[0]
        carry_ref[...] = fn_ref[0]

    qb = q.astype(BF16)
    own = ((lax.broadcasted_iota(jnp.int32, (FOX_H, n_flat), 1) % FOX_H)
           == lax.broadcasted_iota(jnp.int32, (FOX_H, n_flat), 0))
    carry = carry_ref[...]
    f_pages = [fc_ref[0] for fc_ref in fc_refs]
    halves = []
    for f_page in f_pages:
        hi = f_page.astype(BF16).astype(F32)
        halves += [hi, f_page - hi]
    suffix = _dot(jnp.concatenate(halves, axis=0).astype(BF16), u2_ref[...])
    scores = []
    for i, (kc_ref, f_page) in enumerate(zip(kc_refs, f_pages)):
        kf = kc_ref[0].reshape(n_flat, FOX_HD).astype(BF16)
        bias = carry + suffix[2 * i * FOX_H:(2 * i + 1) * FOX_H] + suffix[(2 * i + 1) * FOX_H:(2 * i + 2) * FOX_H]
        scores.append(jnp.where(own, _dot_nt(qb, kf) + bias * LOG2E, -jnp.inf))
        carry = carry + jnp.sum(f_page, axis=-1, keepdims=True)
    carry_ref[...] = carry

    m_prev = m_ref[...]
    m_new = m_prev
    for s in scores:
        m_new = jnp.maximum(m_new, jnp.max(s, axis=-1, keepdims=True))
    alpha = jnp.exp2(m_prev - m_new)
    l_new = alpha * l_ref[...]
    acc = alpha * acc_ref[...]
    for s, vc_ref in zip(scores, vc_refs):
        p = jnp.exp2(s - m_new)
        l_new = l_new + jnp.sum(p, axis=-1, keepdims=True)
        acc = acc + _dot(p.astype(BF16), vc_ref[0].reshape(n_flat, FOX_HD).astype(BF16))
    l_ref[...] = l_new
    acc_ref[...] = acc
    m_ref[...] = m_new

    @pl.when(last)
    def _():
        o_ref[0] = _rms(acc / l_new, gain_ref[...])


def _head_sum(x, ind, ind_t):
    hi, lo = _split(x)
    s = _dot(hi, ind) + _dot(lo, ind)
    s_hi, s_lo = _split(s)
    return _dot(s_hi, ind_t) + _dot(s_lo, ind_t)


def _rwkv_prep_kernel(pr_ref, prev_ref, mu_ref, vec_ref, wwa_ref, wg_ref, ind_ref, indt_ref,
                      r_ref, lw_ref, k_ref, v_ref, kk_ref, b_ref, g_ref, bonus_ref, carry_ref, *, sequential):
    pr = pr_ref[...]
    if sequential:
        @pl.when(pl.program_id(0) == 0)
        def _():
            carry_ref[...] = jnp.zeros_like(carry_ref)

        rolled = pltpu.roll(pr, 1, 0)
        first = lax.broadcasted_iota(jnp.int32, pr.shape, 0) == 0
        prev = jnp.where(first, carry_ref[...], rolled)
        carry_ref[...] = pr[-1:, :]
    else:
        prev = prev_ref[...]
    xs = pr + (prev - pr) * mu_ref[...]
    r = xs[:, 0:RW_W]
    kr = xs[:, RW_W:2 * RW_W]
    vr = xs[:, 2 * RW_W:3 * RW_W]
    wa = xs[:, PR_WA:PR_WA + LANE]
    gd = xs[:, PR_G:PR_W]
    w0, a0, k_k, k_a, r_k = (vec_ref[i:i + 1, :] for i in range(5))

    is_decay = lax.broadcasted_iota(jnp.int32, wa.shape, 1) < R_DECAY
    wa_act = jnp.where(is_decay, jnp.tanh(wa), wa).astype(BF16)
    lora = _dot(wa_act, wwa_ref[...])
    x_w = w0 + lora[:, :RW_W]
    w_log = -(jnp.maximum(-x_w, 0.0) + jnp.log1p(jnp.exp(-jnp.abs(x_w)))) - 0.5
    lw_ref[...] = -jnp.exp(w_log)
    a = jax.nn.sigmoid(a0 + lora[:, RW_W:])
    g_ref[...] = _dot(jax.nn.sigmoid(gd).astype(BF16), wg_ref[...])

    kk = kr * k_k
    ss = _head_sum(kk * kk, ind_ref[...], indt_ref[...])
    kk = kk / jnp.maximum(jnp.sqrt(ss), 1e-12)
    k2 = kr * (1.0 + (a - 1.0) * k_a)
    bonus_ref[...] = _head_sum(r * k2 * r_k, ind_ref[...], indt_ref[...]) * vr
    r_ref[...] = r
    k_ref[...] = k2
    v_ref[...] = vr
    kk_ref[...] = kk
    b_ref[...] = kk * a


def _rwkv_prep(pr, prev, mu_row, vecs, w_wa, w_g, ind, ind_t, tm, sequential):
    t = pr.shape[0]
    row = lambda i: (i, 0)
    const = lambda i: (0, 0)
    out = jax.ShapeDtypeStruct((t, RW_W), F32)
    return pl.pallas_call(
        functools.partial(_rwkv_prep_kernel, sequential=sequential),
        out_shape=(out,) * 8,
        grid=(t // tm,),
        in_specs=[
            pl.BlockSpec((tm, PR_W), row),
            pl.BlockSpec((prev.shape[0] if sequential else tm, PR_W), const if sequential else row),
            pl.BlockSpec((1, PR_W), const),
            pl.BlockSpec((8, RW_W), const),
            pl.BlockSpec((LANE, 2 * RW_W), const),
            pl.BlockSpec((2 * LANE, RW_W), const),
            pl.BlockSpec((RW_W, LANE), const),
            pl.BlockSpec((LANE, RW_W), const),
        ],
        out_specs=(pl.BlockSpec((tm, RW_W), row),) * 8,
        scratch_shapes=[pltpu.VMEM((1, PR_W), F32)],
        compiler_params=_cparams(("arbitrary",)),
        name="rwkv_prep",
    )(pr, prev, mu_row, vecs, w_wa, w_g, ind, ind_t)


def _wkv_chunk_kernel(r_ref, lw_ref, k_ref, v_ref, kk_ref, b_ref, s0_ref, y_ref, s_ref):
    L = WKV_L

    @pl.when(pl.program_id(0) == 0)
    def _():
        s_ref[...] = s0_ref[...]

    lw = lw_ref[...]
    tril = (lax.broadcasted_iota(jnp.int32, (L, L), 0) >= lax.broadcasted_iota(jnp.int32, (L, L), 1)).astype(BF16)
    lw_hi, lw_lo = _split(lw)
    gcum = _dot(tril, lw_hi) + _dot(tril, lw_lo)
    e_g = jnp.exp(gcum)
    e_gi = jnp.exp(-gcum)
    rt = r_ref[...] * e_g
    kt = k_ref[...] * e_gi
    bt = b_ref[...] * e_gi
    kkt = kk_ref[...] * jnp.exp(gcum - lw)
    v = v_ref[...]

    row = lax.broadcasted_iota(jnp.int32, (2 * L, 2 * L), 0)
    col = lax.broadcasted_iota(jnp.int32, (2 * L, 2 * L), 1)
    same = (row // L) == (col // L)
    strict = same & ((col % L) < (row % L))
    incl = same & ((col % L) <= (row % L))
    head0 = lax.broadcasted_iota(jnp.int32, (L, 2 * L), 1) < RW_N

    def bd(x):
        return jnp.concatenate([jnp.where(head0, x, 0.0), jnp.where(head0, 0.0, x)], axis=0).astype(BF16)

    def dup(x):
        xb = x.astype(BF16)
        return jnp.concatenate([xb, xb], axis=0)

    pairs = range(RW_H // 2)
    sls = [slice(p * 2 * RW_N, (p + 1) * 2 * RW_N) for p in pairs]
    kkr_bd = [jnp.concatenate([bd(kkt[:, sl]), bd(rt[:, sl])], axis=0) for sl in sls]
    bk_bd = [jnp.concatenate([bd(bt[:, sl]), bd(kt[:, sl])], axis=0) for sl in sls]
    bk_dup = [jnp.concatenate([dup(bt[:, sl]), dup(kt[:, sl])], axis=0) for sl in sls]
    v_bd = [bd(v[:, sl]) for sl in sls]
    sp = [s_ref[p] for p in pairs]
    a_all = [_dot_nt(kkr_bd[p], bk_dup[p]) for p in pairs]
    from_state = [_dot_nt(kkr_bd[p], sp[p].astype(BF16)) for p in pairs]
    n = [jnp.where(strict, a[:2 * L, :2 * L], 0.0) for a in a_all]
    a_bk = [jnp.where(strict, a[:2 * L, 2 * L:], 0.0).astype(BF16) for a in a_all]
    a_r = [jnp.concatenate([jnp.where(incl, a[2 * L:, :2 * L], 0.0), jnp.where(incl, a[2 * L:, 2 * L:], 0.0)],
                           axis=1).astype(BF16) for a in a_all]
    x = [-(from_state[p][:2 * L] + _dot(a_bk[p], v_bd[p])) for p in pairs]
    x = [x[p] - _mm(n[p], x[p]) for p in pairs]
    steps = 1
    while 2 * steps < L:
        n = [_mm(n[p], n[p]) for p in pairs]
        x = [x[p] + _mm(n[p], x[p]) for p in pairs]
        steps *= 2
    dv = [jnp.concatenate([x[p].astype(BF16), v_bd[p]], axis=0) for p in pairs]
    for p in pairs:
        y_bd = from_state[p][2 * L:] + _dot(a_r[p], dv[p])
        y_ref[:, sls[p]] = y_bd[:L] + y_bd[L:]
        s_ref[p] = (sp[p] + _dot_tn(dv[p], bk_bd[p])) * e_g[L - 1:L, sls[p]]


def _mm(a, b):
    return _dot(a.astype(BF16), b.astype(BF16))


def _wkv_decode_kernel(pt_ref, r_ref, lw_ref, k_ref, v_ref, kk_ref, b_ref, s0_ref, q_ref, kn_ref, vn_ref, fn_ref, *refs,
                       group, steps_per_seq, n_seq, n_chunks):
    del pt_ref
    kc_refs, vc_refs, fc_refs = refs[:group], refs[group:2 * group], refs[2 * group:3 * group]
    u2_ref, gain_ref, y_ref, s_ref, oa_ref, m_ref, l_ref, acc_ref, carry_ref = refs[3 * group:]
    c = pl.program_id(0)
    _wkv_chunk_kernel(r_ref, lw_ref, k_ref, v_ref, kk_ref, b_ref, s0_ref, y_ref, s_ref)

    def decode():
        step = c % steps_per_seq
        _decode_pages(step == 0, step == steps_per_seq - 1, q_ref, kn_ref, vn_ref, fn_ref, kc_refs, vc_refs, fc_refs,
                      u2_ref, gain_ref, oa_ref, m_ref, l_ref, acc_ref, carry_ref)

    if n_seq * steps_per_seq == n_chunks:
        decode()
    else:
        pl.when(c < n_seq * steps_per_seq)(decode)


def _wkv_decode(r, lw, k, v, kk, b, s0_pairs, page_table, q, k_new, v_new, f_new, cache_k, cache_v, cache_ft, gain, layer):
    t = r.shape[0]
    n_chunks = t // WKV_L
    n_seq = q.shape[0]
    n_pages = page_table.shape[1]
    group = min(g for g in range(1, n_pages + 1) if n_pages % g == 0 and n_seq * (n_pages // g) <= n_chunks)
    assert group <= MAX_PAGES_PER_STEP
    steps_per_seq = n_pages // group
    attn_steps = n_seq * steps_per_seq
    last = n_pages - 1
    n_flat = PAGE_SIZE * FOX_H
    n_pairs = RW_H // 2
    u2 = (jnp.arange(PAGE_SIZE)[:, None] > (jnp.arange(n_flat) // FOX_H)[None, :]).astype(BF16)

    def seq_of(c):
        return jnp.minimum(c, attn_steps - 1) // steps_per_seq

    def page_of(c, pt, g):
        return pt[seq_of(c), last - ((jnp.minimum(c, attn_steps - 1) % steps_per_seq) * group + g)]

    rows = pl.BlockSpec((WKV_L, RW_W), lambda c, pt: (c, 0))
    sspec = pl.BlockSpec((n_pairs, LANE, LANE), lambda c, pt: (0, 0, 0))
    tok = pl.BlockSpec((1, FOX_H, FOX_HD), lambda c, pt: (seq_of(c), 0, 0))

    def page_kv(g):
        return pl.BlockSpec((None, 1, PAGE_SIZE, FOX_H, FOX_HD), lambda c, pt: (layer, page_of(c, pt, g), 0, 0, 0))

    def page_f(g):
        return pl.BlockSpec((None, 1, FOX_H, PAGE_SIZE), lambda c, pt: (layer, page_of(c, pt, g), 0, 0))

    stat = pltpu.VMEM((FOX_H, 1), F32)
    grid_spec = pltpu.PrefetchScalarGridSpec(
        num_scalar_prefetch=1,
        grid=(n_chunks,),
        in_specs=[
            rows, rows, rows, rows, rows, rows, sspec,
            tok, tok, tok,
            pl.BlockSpec((1, FOX_H, 1), lambda c, pt: (seq_of(c), 0, 0)),
            *[page_kv(g) for g in range(group)], *[page_kv(g) for g in range(group)],
            *[page_f(g) for g in range(group)],
            pl.BlockSpec((PAGE_SIZE, n_flat), lambda c, pt: (0, 0)),
            pl.BlockSpec((FOX_H, FOX_HD), lambda c, pt: (0, 0)),
        ],
        out_specs=(rows, sspec, tok),
        scratch_shapes=[stat, stat, pltpu.VMEM((FOX_H, FOX_HD), F32), stat],
    )
    return pl.pallas_call(
        functools.partial(_wkv_decode_kernel, group=group, steps_per_seq=steps_per_seq, n_seq=n_seq, n_chunks=n_chunks),
        out_shape=(jax.ShapeDtypeStruct((t, RW_W), F32), jax.ShapeDtypeStruct((n_pairs, LANE, LANE), F32),
                   jax.ShapeDtypeStruct((n_seq, FOX_H, FOX_HD), F32)),
        grid_spec=grid_spec,
        compiler_params=_cparams(("arbitrary",)),
        name="wkv_decode",
    )(page_table, r, lw, k, v, kk, b, s0_pairs, q, k_new, v_new, f_new, *([cache_k] * group), *([cache_v] * group),
      *([cache_ft] * group), u2, gain)


def _wkv_step_kernel(s_ref, r_ref, lw_ref, k_ref, v_ref, kk_ref, b_ref, y_ref, so_ref):
    s = s_ref[...]
    s_kk = jnp.sum(s * kk_ref[...], axis=-1, keepdims=True)
    s = s * jnp.exp(lw_ref[...]) - s_kk * b_ref[...] + v_ref[...] * k_ref[...]
    so_ref[...] = s
    y_ref[...] = jnp.sum(s * r_ref[...], axis=-1, keepdims=True)


def _wkv_step(state, r, lw, k, v, kk, b):
    bsz = state.shape[0]
    nb = 4 if bsz % 4 == 0 else 1
    keyed = lambda x: x.reshape(bsz, RW_H, 1, RW_N)
    kspec = pl.BlockSpec((nb, RW_H, 1, RW_N), lambda i: (i, 0, 0, 0))
    vspec = pl.BlockSpec((nb, RW_H, RW_N, 1), lambda i: (i, 0, 0, 0))
    sspec = pl.BlockSpec((nb, RW_H, RW_N, RW_N), lambda i: (i, 0, 0, 0))
    y, s_new = pl.pallas_call(
        _wkv_step_kernel,
        out_shape=(jax.ShapeDtypeStruct((bsz, RW_H, RW_N, 1), F32), jax.ShapeDtypeStruct(state.shape, F32)),
        grid=(bsz // nb,),
        in_specs=[sspec, kspec, kspec, kspec, vspec, kspec, kspec],
        out_specs=(vspec, sspec),
        compiler_params=_cparams(("parallel",)),
        name="wkv_step",
    )(state, keyed(r), keyed(lw), keyed(k), v.reshape(bsz, RW_H, RW_N, 1), keyed(kk), keyed(b))
    return y.reshape(bsz, RW_W), s_new


def _mix_out_kernel(x_ref, of_ref, y_ref, bonus_ref, g_ref, lnw_ref, lnb_ref, ind_ref, indt_ref, wo_ref, o_ref):
    y = y_ref[...]
    ind, ind_t = ind_ref[...], indt_ref[...]
    mean = _head_sum(y, ind, ind_t) * (1.0 / RW_N)
    yc = y - mean
    var = _head_sum(yc * yc, ind, ind_t) * (1.0 / RW_N)
    yn = yc * lax.rsqrt(var + LNX_EPS) * lnw_ref[...] + lnb_ref[...]
    o_rw = ((yn + bonus_ref[...]) * g_ref[...]).astype(BF16)
    o_ref[...] = (x_ref[...] + _dot(of_ref[...].astype(BF16), wo_ref[:FOX_W, :]) + _dot(o_rw, wo_ref[FOX_W:, :]))


def _mix_out(x, o_fox, y, bonus, g, lnw_row, lnb_row, ind, ind_t, w_out, tm):
    t, d = x.shape
    row = lambda i: (i, 0)
    const = lambda i: (0, 0)
    half = pl.BlockSpec((tm, RW_W), row)
    return pl.pallas_call(
        _mix_out_kernel,
        out_shape=jax.ShapeDtypeStruct((t, d), F32),
        grid=(t // tm,),
        in_specs=[
            pl.BlockSpec((tm, d), row), pl.BlockSpec((tm, FOX_W), row), half, half, half,
            pl.BlockSpec((1, RW_W), const), pl.BlockSpec((1, RW_W), const),
            pl.BlockSpec((RW_W, LANE), const), pl.BlockSpec((LANE, RW_W), const),
            pl.BlockSpec((d, d), const),
        ],
        out_specs=pl.BlockSpec((tm, d), row),
        compiler_params=_cparams(("parallel",)),
        name="mix_out",
    )(x, o_fox, y, bonus, g, lnw_row, lnb_row, ind, ind_t, w_out)


def _final_norm_kernel(x_ref, g_ref, o_ref):
    o_ref[...] = _rms(x_ref[...], g_ref[...])


def _final_norm(x, g, tm):
    t, d = x.shape
    return pl.pallas_call(
        _final_norm_kernel,
        out_shape=jax.ShapeDtypeStruct((t, d), F32),
        grid=(t // tm,),
        in_specs=[pl.BlockSpec((tm, d), lambda i: (i, 0)), pl.BlockSpec((1, d), lambda i: (0, 0))],
        out_specs=pl.BlockSpec((tm, d), lambda i: (i, 0)),
        compiler_params=_cparams(("parallel",)),
        name="final_norm",
    )(x, g)


def _pack_cols(a):
    lead = a.shape[:-1]
    z = lambda n: jnp.zeros(lead + (n,), a.dtype)
    return jnp.concatenate([a[..., :PR_WA + LANE], z(LANE), a[..., PR_WA + LANE:], z(2 * LANE - R_G)], axis=-1)


def _unpack_cols(a):
    return jnp.concatenate([a[..., :PR_WA + LANE], a[..., PR_G:PR_G + R_G]], axis=-1)


def _layer_weights(l, norm_ffa, ffa_w1, ffa_w3, ffa_w2, norm_mix, w_in, b_f, fox_gain, mu_shift, w0, w_up,
                   a0, a_up, g_up, k_k, k_a, r_k, lnx_w, lnx_b, w_out, norm_ffb, ffb_w1, ffb_w3, ffb_w2):
    d = D_MODEL
    w = w_in[l]
    pad = lambda n: jnp.zeros((d, n), F32)
    w_all = jnp.concatenate([
        w[:, FOX_W:3 * FOX_W], w[:, :FOX_W],
        w[:, C_FOX_IN:C_FOX_IN + PR_F], w[:, 3 * FOX_W:C_FOX_IN], pad(LANE - FOX_H),
        w[:, C_FOX_IN + PR_F:], pad(2 * LANE - R_G)], axis=1).astype(BF16)
    zeros = jnp.zeros((R_DECAY, RW_W), F32)
    w_wa = jnp.concatenate([jnp.concatenate([w_up[l], zeros], axis=1), jnp.concatenate([zeros, a_up[l]], axis=1)], axis=0)
    w_g = jnp.concatenate([g_up[l], jnp.zeros((2 * LANE - R_G, RW_W), F32)], axis=0)
    vecs = jnp.stack([w0[l], a0[l], k_k[l], k_a[l], r_k[l].reshape(RW_W)] + [jnp.zeros((RW_W,), F32)] * 3)
    return dict(
        norm_ffa=norm_ffa[l].reshape(1, d), ffa_w1=ffa_w1[l].astype(BF16), ffa_w3=ffa_w3[l].astype(BF16),
        ffa_w2=ffa_w2[l].astype(BF16),
        norm_mix=norm_mix[l].reshape(1, d), w_all=w_all,
        bf_row=jnp.pad(b_f[l], (0, LANE - FOX_H)).reshape(1, LANE),
        gain_row=fox_gain[l].reshape(1, FOX_W), mu_row=_pack_cols(mu_shift[l]).reshape(1, PR_W),
        vecs=vecs, w_wa=w_wa.astype(BF16), w_g=w_g.astype(BF16),
        lnw_row=lnx_w[l].reshape(1, RW_W), lnb_row=lnx_b[l].reshape(1, RW_W), w_out=w_out[l].astype(BF16),
        norm_ffb=norm_ffb[l].reshape(1, d), ffb_w1=ffb_w1[l].astype(BF16), ffb_w3=ffb_w3[l].astype(BF16),
        ffb_w2=ffb_w2[l].astype(BF16),
    )


def _head_indicators():
    lane_head = jnp.arange(RW_W) // RW_N
    ind = (lane_head[:, None] == jnp.arange(LANE)[None, :]).astype(BF16)
    return ind, ind.T


def _pairs_from_state(s):
    s = s.reshape(RW_H // 2, 2, RW_N, RW_N)
    z = jnp.zeros_like(s[:, 0])
    return jnp.concatenate([jnp.concatenate([s[:, 0], z], axis=2), jnp.concatenate([z, s[:, 1]], axis=2)], axis=1)


def _state_from_pairs(sp):
    return jnp.stack([sp[:, :RW_N, :RW_N], sp[:, RW_N:, RW_N:]], axis=1).reshape(RW_H, RW_N, RW_N)


def _tile(t, pref):
    return pref if t % pref == 0 else t


def _layer(xp, xs, lw_, ind, ind_t, layer, cache_k, cache_v, cache_ft, page_table, state, shift_prev):
    t, bsz = xp.shape[0], xs.shape[0]
    tm = _tile(t, 512)
    ffa = (lw_["norm_ffa"], lw_["ffa_w1"], lw_["ffa_w3"], lw_["ffa_w2"])
    ffb = (lw_["norm_ffb"], lw_["ffb_w1"], lw_["ffb_w3"], lw_["ffb_w2"])

    xs = _ffn(xs, *ffa, bsz, 1408)
    k_s, v_s, qkvb_s = _proj_qkv(xs, lw_["norm_mix"], lw_["w_all"], bsz)
    pr_s = _norm_matmul(xs, lw_["norm_mix"], lw_["w_all"], bsz, PR_TN, 3 * FOX_W, PR_W)
    logf_s = _logf_only(pr_s, lw_["bf_row"])
    q_s = qkvb_s[:, 2 * FOX_W:].reshape(bsz, FOX_H, FOX_HD)

    xp = _ffn(xp, *ffa, tm, 512)
    k_p, v_p, qkvb = _proj_qkv(xp, lw_["norm_mix"], lw_["w_all"], tm)
    pr = _norm_matmul(xp, lw_["norm_mix"], lw_["w_all"], tm, PR_TN, 3 * FOX_W, PR_W)
    logf, c, ct = _logf(pr, lw_["bf_row"], _tile(t, 256))
    o_fox = _fox_prompt(qkvb, c, ct, lw_["gain_row"], tm)
    r, lw, k, v, kk, b, g, bonus = _rwkv_prep(pr, lw_["mu_row"], lw_["mu_row"], lw_["vecs"], lw_["w_wa"], lw_["w_g"],
                                              ind, ind_t, _tile(t, 256), True)
    y, s_pairs, o_fox_s = _wkv_decode(r, lw, k, v, kk, b, jnp.zeros((RW_H // 2, LANE, LANE), F32), page_table, q_s, k_s,
                                      v_s, logf_s.reshape(bsz, FOX_H, 1), cache_k, cache_v, cache_ft,
                                      lw_["gain_row"].reshape(FOX_H, FOX_HD), layer)
    xp = _mix_out(xp, o_fox, y, bonus, g, lw_["lnw_row"], lw_["lnb_row"], ind, ind_t, lw_["w_out"], _tile(t, 256))
    xp = _ffn(xp, *ffb, tm, 512)
    out_p = (k_p[None], v_p[None], logf.reshape(1, t, FOX_H), _state_from_pairs(s_pairs)[None],
             _unpack_cols(pr[t - 1:t, :]))

    r, lw, k, v, kk, b, g, bonus = _rwkv_prep(pr_s, _pack_cols(shift_prev), lw_["mu_row"], lw_["vecs"], lw_["w_wa"],
                                              lw_["w_g"], ind, ind_t, bsz, False)
    y, s_new = _wkv_step(state, r, lw, k, v, kk, b)
    xs = _mix_out(xs, o_fox_s.reshape(bsz, FOX_W), y, bonus, g, lw_["lnw_row"], lw_["lnb_row"], ind, ind_t,
                  lw_["w_out"], bsz)
    xs = _ffn(xs, *ffb, bsz, 1408)
    out_s = (k_s.reshape(bsz, 1, FOX_H, FOX_HD), v_s.reshape(bsz, 1, FOX_H, FOX_HD), logf_s.reshape(bsz, 1, FOX_H),
             s_new, _unpack_cols(pr_s))
    return xp, xs, out_p, out_s


def kernel(x_prompt, x_sample, cache_k, cache_v, cache_logf, state_rwkv, state_shift, page_table, norm_ffa, ffa_w1, ffa_w3, ffa_w2, norm_mix, w_in, b_f, fox_gain, mu_shift, w0, w_up, a0, a_up, g_up, k_k, k_a, r_k, lnx_w, lnx_b, w_out, norm_ffb, ffb_w1, ffb_w3, ffb_w2, norm_final):
    depth = norm_ffa.shape[0]
    b_p, seq, d = x_prompt.shape
    b_s = x_sample.shape[0]
    assert b_p == 1 and x_sample.shape[1] == 1 and seq % WKV_L == 0
    cache_ft = jnp.swapaxes(cache_logf, 2, 3)
    ind, ind_t = _head_indicators()
    xp = x_prompt.reshape(seq, d)
    xs = x_sample.reshape(b_s, d)
    outs_p, outs_s = [], []
    for l in range(depth):
        lw_ = _layer_weights(l, norm_ffa, ffa_w1, ffa_w3, ffa_w2, norm_mix, w_in, b_f, fox_gain, mu_shift, w0, w_up,
                             a0, a_up, g_up, k_k, k_a, r_k, lnx_w, lnx_b, w_out, norm_ffb, ffb_w1, ffb_w3, ffb_w2)
        xp, xs, out_p, out_s = _layer(xp, xs, lw_, ind, ind_t, l, cache_k, cache_v, cache_ft, page_table,
                                      state_rwkv[l], state_shift[l])
        outs_p.append(out_p)
        outs_s.append(out_s)
    g_final = norm_final.reshape(1, d)
    y_prompt = _final_norm(xp, g_final, _tile(seq, 512)).reshape(b_p, seq, d)
    y_sample = _final_norm(xs, g_final, b_s).reshape(b_s, 1, d)
    stack = lambda outs, i: jnp.stack([o[i] for o in outs])
    return (y_prompt, y_sample,
            stack(outs_p, 0), stack(outs_p, 1), stack(outs_p, 2), stack(outs_p, 3), stack(outs_p, 4),
            stack(outs_s, 0), stack(outs_s, 1), stack(outs_s, 2), stack(outs_s, 3), stack(outs_s, 4))
```

```python
import functools

import jax
import jax.numpy as jnp
from jax import lax
from jax.experimental import pallas as pl
from jax.experimental.pallas import tpu as pltpu

F32 = jnp.float32
BF16 = jnp.bfloat16

D_MODEL = 2048
FOX_HD = 128
FOX_W = D_MODEL // 2
FOX_H = FOX_W // FOX_HD
RW_N = 64
RW_W = D_MODEL - FOX_W
RW_H = RW_W // RW_N
R_DECAY = 64
R_A = 64
R_G = 160
PAGE_SIZE = 128
C_FOX_IN = 3 * FOX_W + FOX_H
C_SHIFT = 3 * RW_W + R_DECAY + R_A + R_G
RMS_EPS = 1e-6
LNX_EPS = 64e-5
FFN_RES = 0.5

LOG2E = 1.4426950408889634
Q_SCALE = FOX_HD ** -0.5 * LOG2E
LANE = 128
PR_WA = 3 * RW_W
PR_F = PR_WA + LANE
PR_G = PR_F + LANE
PR_W = PR_G + 2 * LANE
PR_TN = 512
WKV_L = 64
HEAD_UNROLL = 4
MAX_PAGES_PER_STEP = 16
VMEM_LIMIT = 56 * 1024 * 1024


def _cparams(sem):
    return pltpu.CompilerParams(dimension_semantics=sem, vmem_limit_bytes=VMEM_LIMIT)


def _dot(a, b):
    return jnp.dot(a, b, preferred_element_type=F32)


def _dot_nt(a, b):
    return lax.dot_general(a, b, (((1,), (1,)), ((), ())), preferred_element_type=F32)


def _dot_tn(a, b):
    return lax.dot_general(a, b, (((0,), (0,)), ((), ())), preferred_element_type=F32)


def _split(x):
    hi = x.astype(BF16)
    lo = (x - hi.astype(F32)).astype(BF16)
    return hi, lo


def _rms(x, g):
    return x * lax.rsqrt(jnp.mean(x * x, axis=-1, keepdims=True) + RMS_EPS) * g


def _ffn_kernel(x_ref, g_ref, w1_ref, w3_ref, w2_ref, o_ref, hn_ref):
    @pl.when(pl.program_id(1) == 0)
    def _():
        x = x_ref[...]
        hn_ref[...] = _rms(x, g_ref[...]).astype(BF16)
        o_ref[...] = x

    hn = hn_ref[...]
    h1 = _dot(hn, w1_ref[...])
    h3 = _dot(hn, w3_ref[...])
    a = (h1 * jax.nn.sigmoid(h1) * h3 * FFN_RES).astype(BF16)
    o_ref[...] += _dot(a, w2_ref[...])


def _ffn(x, g, w1, w3, w2, tm, tf):
    t, d = x.shape
    f = w1.shape[1]
    return pl.pallas_call(
        _ffn_kernel,
        out_shape=jax.ShapeDtypeStruct((t, d), F32),
        grid=(t // tm, f // tf),
        in_specs=[
            pl.BlockSpec((tm, d), lambda i, j: (i, 0)),
            pl.BlockSpec((1, d), lambda i, j: (0, 0)),
            pl.BlockSpec((d, tf), lambda i, j: (0, j)),
            pl.BlockSpec((d, tf), lambda i, j: (0, j)),
            pl.BlockSpec((tf, d), lambda i, j: (j, 0)),
        ],
        out_specs=pl.BlockSpec((tm, d), lambda i, j: (i, 0)),
        scratch_shapes=[pltpu.VMEM((tm, d), BF16)],
        compiler_params=_cparams(("parallel", "arbitrary")),
        name="ffn",
    )(x, g, w1, w3, w2)


def _ffn_cast_kernel(x_ref, g_ref, w1_ref, w3_ref, w2_ref, o_ref, w1b_ref, w3b_ref, w2b_ref, hn_ref):
    w1b_ref[...] = w1_ref[...].astype(BF16)
    w3b_ref[...] = w3_ref[...].astype(BF16)
    w2b_ref[...] = w2_ref[...].astype(BF16)
    _ffn_kernel(x_ref, g_ref, w1b_ref, w3b_ref, w2b_ref, o_ref, hn_ref)


def _ffn_cast(x, g, w1, w3, w2, tf):
    t, d = x.shape
    f = w1.shape[1]
    up = pl.BlockSpec((d, tf), lambda i, j: (0, j))
    down = pl.BlockSpec((tf, d), lambda i, j: (j, 0))
    return pl.pallas_call(
        _ffn_cast_kernel,
        out_shape=(jax.ShapeDtypeStruct((t, d), F32), jax.ShapeDtypeStruct((d, f), BF16),
                   jax.ShapeDtypeStruct((d, f), BF16), jax.ShapeDtypeStruct((f, d), BF16)),
        grid=(1, f // tf),
        in_specs=[pl.BlockSpec((t, d), lambda i, j: (0, 0)), pl.BlockSpec((1, d), lambda i, j: (0, 0)), up, up, down],
        out_specs=(pl.BlockSpec((t, d), lambda i, j: (0, 0)), up, up, down),
        scratch_shapes=[pltpu.VMEM((t, d), BF16)],
        compiler_params=_cparams(("arbitrary", "arbitrary")),
        name="ffn_cast",
    )(x, g, w1, w3, w2)


def _norm_matmul_kernel(x_ref, g_ref, w_ref, o_ref, hn_ref):
    @pl.when(pl.program_id(1) == 0)
    def _():
        hn_ref[...] = _rms(x_ref[...], g_ref[...]).astype(BF16)

    o_ref[...] = _dot(hn_ref[...], w_ref[...])


def _norm_matmul(x, g, w, tm, tn, col0, n):
    t, d = x.shape
    first = col0 // tn
    assert col0 % tn == 0 and n % tn == 0
    return pl.pallas_call(
        _norm_matmul_kernel,
        out_shape=jax.ShapeDtypeStruct((t, n), F32),
        grid=(t // tm, n // tn),
        in_specs=[
            pl.BlockSpec((tm, d), lambda i, j: (i, 0)),
            pl.BlockSpec((1, d), lambda i, j: (0, 0)),
            pl.BlockSpec((d, tn), lambda i, j: (0, first + j)),
        ],
        out_specs=pl.BlockSpec((tm, tn), lambda i, j: (i, j)),
        scratch_shapes=[pltpu.VMEM((tm, d), BF16)],
        compiler_params=_cparams(("parallel", "arbitrary")),
        name="norm_matmul",
    )(x, g, w)


def _proj_qkv_kernel(x_ref, g_ref, w_ref, k_ref, v_ref, qkvb_ref, hn_ref):
    j = pl.program_id(1)

    @pl.when(j == 0)
    def _():
        hn_ref[...] = _rms(x_ref[...], g_ref[...]).astype(BF16)

    res = _dot(hn_ref[...], w_ref[...])

    def emit(head_ref):
        for h in range(FOX_H):
            head_ref[:, h, :] = res[:, h * FOX_HD:(h + 1) * FOX_HD]
        qkvb_ref[...] = res.astype(BF16)

    pl.when(j == 0)(lambda: emit(k_ref))
    pl.when(j == 1)(lambda: emit(v_ref))

    @pl.when(j == 2)
    def _():
        qkvb_ref[...] = (res * Q_SCALE).astype(BF16)


def _proj_qkv(x, g, w_kvq, tm):
    t, d = x.shape
    heads = jax.ShapeDtypeStruct((t, FOX_H, FOX_HD), F32)
    hspec = pl.BlockSpec((tm, FOX_H, FOX_HD), lambda i, j: (i, 0, 0))
    return pl.pallas_call(
        _proj_qkv_kernel,
        out_shape=(heads, heads, jax.ShapeDtypeStruct((t, 3 * FOX_W), BF16)),
        grid=(t // tm, 3),
        in_specs=[
            pl.BlockSpec((tm, d), lambda i, j: (i, 0)),
            pl.BlockSpec((1, d), lambda i, j: (0, 0)),
            pl.BlockSpec((d, FOX_W), lambda i, j: (0, j)),
        ],
        out_specs=(hspec, hspec, pl.BlockSpec((tm, FOX_W), lambda i, j: (i, j))),
        scratch_shapes=[pltpu.VMEM((tm, d), BF16)],
        compiler_params=_cparams(("parallel", "arbitrary")),
        name="proj_qkv",
    )(x, g, w_kvq)


def _log_sigmoid(z):
    return jnp.minimum(z, 0.0) - jnp.log1p(jnp.exp(-jnp.abs(z)))


def _logf_kernel(f_ref, bf_ref, tri_ref, lf_ref, c_ref, ct_ref, carry_ref):
    @pl.when(pl.program_id(0) == 0)
    def _():
        carry_ref[...] = jnp.zeros_like(carry_ref)

    lf = _log_sigmoid(f_ref[...] + bf_ref[...])
    hi, lo = _split(lf)
    c = _dot(tri_ref[...], hi) + _dot(tri_ref[...], lo) + carry_ref[...]
    carry_ref[...] = c[-1:, :]
    lf_ref[...] = lf[:, :FOX_H]
    c2 = c * LOG2E
    c_ref[...] = c2[:, :FOX_H]
    ct_ref[...] = c2.T[:FOX_H, :]


def _logf(pr, bf_row, tm):
    t = pr.shape[0]
    tri = (lax.broadcasted_iota(jnp.int32, (tm, tm), 0) >= lax.broadcasted_iota(jnp.int32, (tm, tm), 1)).astype(BF16)
    return pl.pallas_call(
        _logf_kernel,
        out_shape=(
            jax.ShapeDtypeStruct((t, FOX_H), F32),
            jax.ShapeDtypeStruct((t, FOX_H), F32),
            jax.ShapeDtypeStruct((FOX_H, t), F32),
        ),
        grid=(t // tm,),
        in_specs=[
            pl.BlockSpec((tm, LANE), lambda i: (i, PR_F // LANE)),
            pl.BlockSpec((1, LANE), lambda i: (0, 0)),
            pl.BlockSpec((tm, tm), lambda i: (0, 0)),
        ],
        out_specs=(
            pl.BlockSpec((tm, FOX_H), lambda i: (i, 0)),
            pl.BlockSpec((tm, FOX_H), lambda i: (i, 0)),
            pl.BlockSpec((FOX_H, tm), lambda i: (0, i)),
        ),
        scratch_shapes=[pltpu.VMEM((1, LANE), F32)],
        compiler_params=_cparams(("arbitrary",)),
        name="logf_cumsum",
    )(pr, bf_row, tri)


def _logf_only_kernel(f_ref, bf_ref, lf_ref):
    lf_ref[...] = _log_sigmoid(f_ref[...] + bf_ref[...])[:, :FOX_H]


def _logf_only(pr, bf_row):
    t = pr.shape[0]
    return pl.pallas_call(
        _logf_only_kernel,
        out_shape=jax.ShapeDtypeStruct((t, FOX_H), F32),
        grid=(1,),
        in_specs=[
            pl.BlockSpec((t, LANE), lambda i: (0, PR_F // LANE)),
            pl.BlockSpec((1, LANE), lambda i: (0, 0)),
        ],
        out_specs=pl.BlockSpec((t, FOX_H), lambda i: (0, 0)),
        name="logf",
    )(pr, bf_row)


def _fox_prompt_kernel(q_ref, k_ref, v_ref, cq_ref, ck_ref, gain_ref, o_ref, m_ref, l_ref, acc_ref, cqc_ref, *, tq):
    qi = pl.program_id(0)
    ki = pl.program_id(1)

    @pl.when(ki == 0)
    def _():
        m_ref[...] = jnp.full_like(m_ref, -jnp.inf)
        l_ref[...] = jnp.zeros_like(l_ref)
        acc_ref[...] = jnp.zeros_like(acc_ref)
        cq = cq_ref[...]
        for h in range(FOX_H):
            cqc_ref[h] = jnp.broadcast_to(cq[:, h:h + 1], (tq, LANE))

    def step(diagonal):
        tk = k_ref.shape[0]
        if diagonal:
            keep = (lax.broadcasted_iota(jnp.int32, (tq, tq), 0) >= lax.broadcasted_iota(jnp.int32, (tq, tq), 1))
        ones = jnp.ones((tk, LANE), BF16)

        def head(h, carry):
            sl = pl.ds(pl.multiple_of(h * FOX_HD, FOX_HD), FOX_HD)
            s = _dot_nt(q_ref[:, sl], k_ref[:, sl]) - ck_ref[pl.ds(h, 1), :]
            if diagonal:
                s = jnp.where(keep, s, -jnp.inf)
            cq = cqc_ref[h]
            m_prev = m_ref[h]
            m_new = jnp.maximum(m_prev, jnp.max(s, axis=-1, keepdims=True) + cq)
            shift = m_new - cq
            p = jnp.exp2(s - jnp.concatenate([shift] * (tk // LANE), axis=1)).astype(BF16)
            alpha = jnp.exp2(m_prev - m_new)
            pv = _dot(p, jnp.concatenate([v_ref[:, sl], ones], axis=1))
            l_ref[h] = alpha * l_ref[h] + pv[:, FOX_HD:]
            acc_ref[:, sl] = alpha * acc_ref[:, sl] + pv[:, :FOX_HD]
            m_ref[h] = m_new
            return carry

        lax.fori_loop(0, FOX_H, head, 0, unroll=HEAD_UNROLL)

    @pl.when(ki < qi)
    def _():
        step(False)

    @pl.when(ki == qi)
    def _():
        step(True)
        for h in range(FOX_H):
            sl = slice(h * FOX_HD, (h + 1) * FOX_HD)
            o = acc_ref[:, sl] / l_ref[h]
            o_ref[:, sl] = _rms(o, gain_ref[:, sl])


def _fox_prompt(qkvb, c, ct, gain_row, tq):
    t = qkvb.shape[0]
    n = t // tq
    stat = pltpu.VMEM((FOX_H, tq, LANE), F32)
    return pl.pallas_call(
        functools.partial(_fox_prompt_kernel, tq=tq),
        out_shape=jax.ShapeDtypeStruct((t, FOX_W), F32),
        grid=(n, n),
        in_specs=[
            pl.BlockSpec((tq, FOX_W), lambda i, j: (i, 2)),
            pl.BlockSpec((tq, FOX_W), lambda i, j: (jnp.minimum(i, j), 0)),
            pl.BlockSpec((tq, FOX_W), lambda i, j: (jnp.minimum(i, j), 1)),
            pl.BlockSpec((tq, FOX_H), lambda i, j: (i, 0)),
            pl.BlockSpec((FOX_H, tq), lambda i, j: (0, jnp.minimum(i, j))),
            pl.BlockSpec((1, FOX_W), lambda i, j: (0, 0)),
        ],
        out_specs=pl.BlockSpec((tq, FOX_W), lambda i, j: (i, 0)),
        scratch_shapes=[stat, stat, pltpu.VMEM((tq, FOX_W), F32), stat],
        compiler_params=_cparams(("parallel", "arbitrary")),
        name="fox_prompt",
    )(qkvb, qkvb, qkvb, c, ct, gain_row)


def _decode_pages(first, last, q_ref, kn_ref, vn_ref, fn_ref, kc_refs, vc_refs, fc_refs, u2_ref, gain_ref, o_ref,
                  m_ref, l_ref, acc_ref, carry_ref):
    n_flat = PAGE_SIZE * FOX_H
    q = q_ref[0].astype(F32)

    @pl.when(first)
    def _():
        m_ref[...] = jnp.sum(q * kn_ref[0], axis=-1, keepdims=True)
        l_ref[...] = jnp.ones_like(l_ref)
        acc_ref[...] = vn_ref[0]
        carry_ref[...] = fn_ref[0]

    qb = q.astype(BF16)
    own = ((lax.broadcasted_iota(jnp.int32, (FOX_H, n_flat), 1) % FOX_H)
           == lax.broadcasted_iota(jnp.int32, (FOX_H, n_flat), 0))
    carry = carry_ref[...]
    f_pages = [fc_ref[0] for fc_ref in fc_refs]
    halves = []
    for f_page in f_pages:
        hi = f_page.astype(BF16).astype(F32)
        halves += [hi, f_page - hi]
    suffix = _dot(jnp.concatenate(halves, axis=0).astype(BF16), u2_ref[...])
    scores = []
    for i, (kc_ref, f_page) in enumerate(zip(kc_refs, f_pages)):
        kf = kc_ref[0].reshape(n_flat, FOX_HD).astype(BF16)
        bias = carry + suffix[2 * i * FOX_H:(2 * i + 1) * FOX_H] + suffix[(2 * i + 1) * FOX_H:(2 * i + 2) * FOX_H]
        scores.append(jnp.where(own, _dot_nt(qb, kf) + bias * LOG2E, -jnp.inf))
        carry = carry + jnp.sum(f_page, axis=-1, keepdims=True)
    carry_ref[...] = carry

    m_prev = m_ref[...]
    m_new = m_prev
    for s in scores:
        m_new = jnp.maximum(m_new, jnp.max(s, axis=-1, keepdims=True))
    alpha = jnp.exp2(m_prev - m_new)
    l_new = alpha * l_ref[...]
    acc = alpha * acc_ref[...]
    for s, vc_ref in zip(scores, vc_refs):
        p = jnp.exp2(s - m_new)
        l_new = l_new + jnp.sum(p, axis=-1, keepdims=True)
        acc = acc + _dot(p.astype(BF16), vc_ref[0].reshape(n_flat, FOX_HD).astype(BF16))
    l_ref[...] = l_new
    acc_ref[...] = acc
    m_ref[...] = m_new

    @pl.when(last)
    def _():
        o_ref[0] = _rms(acc / l_new, gain_ref[...])


def _head_sum(x, ind, ind_t):
    hi, lo = _split(x)
    s = _dot(hi, ind) + _dot(lo, ind)
    s_hi, s_lo = _split(s)
    return _dot(s_hi, ind_t) + _dot(s_lo, ind_t)


def _rwkv_prep_kernel(pr_ref, prev_ref, mu_ref, vec_ref, wwa_ref, wg_ref, ind_ref, indt_ref,
                      r_ref, lw_ref, k_ref, v_ref, kk_ref, b_ref, g_ref, bonus_ref, carry_ref, *, sequential):
    pr = pr_ref[...]
    if sequential:
        @pl.when(pl.program_id(0) == 0)
        def _():
            carry_ref[...] = jnp.zeros_like(carry_ref)

        rolled = pltpu.roll(pr, 1, 0)
        first = lax.broadcasted_iota(jnp.int32, pr.shape, 0) == 0
        prev = jnp.where(first, carry_ref[...], rolled)
        carry_ref[...] = pr[-1:, :]
    else:
        prev = prev_ref[...]
    xs = pr + (prev - pr) * mu_ref[...]
    r = xs[:, 0:RW_W]
    kr = xs[:, RW_W:2 * RW_W]
    vr = xs[:, 2 * RW_W:3 * RW_W]
    wa = xs[:, PR_WA:PR_WA + LANE]
    gd = xs[:, PR_G:PR_W]
    w0, a0, k_k, k_a, r_k = (vec_ref[i:i + 1, :] for i in range(5))

    is_decay = lax.broadcasted_iota(jnp.int32, wa.shape, 1) < R_DECAY
    wa_act = jnp.where(is_decay, jnp.tanh(wa), wa).astype(BF16)
    lora = _dot(wa_act, wwa_ref[...])
    x_w = w0 + lora[:, :RW_W]
    w_log = -(jnp.maximum(-x_w, 0.0) + jnp.log1p(jnp.exp(-jnp.abs(x_w)))) - 0.5
    lw_ref[...] = -jnp.exp(w_log)
    a = jax.nn.sigmoid(a0 + lora[:, RW_W:])
    g_ref[...] = _dot(jax.nn.sigmoid(gd).astype(BF16), wg_ref[...])

    kk = kr * k_k
    ss = _head_sum(kk * kk, ind_ref[...], indt_ref[...])
    kk = kk / jnp.maximum(jnp.sqrt(ss), 1e-12)
    k2 = kr * (1.0 + (a - 1.0) * k_a)
    bonus_ref[...] = _head_sum(r * k2 * r_k, ind_ref[...], indt_ref[...]) * vr
    r_ref[...] = r
    k_ref[...] = k2
    v_ref[...] = vr
    kk_ref[...] = kk
    b_ref[...] = kk * a


def _rwkv_prep(pr, prev, mu_row, vecs, w_wa, w_g, ind, ind_t, tm, sequential):
    t = pr.shape[0]
    row = lambda i: (i, 0)
    const = lambda i: (0, 0)
    out = jax.ShapeDtypeStruct((t, RW_W), F32)
    return pl.pallas_call(
        functools.partial(_rwkv_prep_kernel, sequential=sequential),
        out_shape=(out,) * 8,
        grid=(t // tm,),
        in_specs=[
            pl.BlockSpec((tm, PR_W), row),
            pl.BlockSpec((prev.shape[0] if sequential else tm, PR_W), const if sequential else row),
            pl.BlockSpec((1, PR_W), const),
            pl.BlockSpec((8, RW_W), const),
            pl.BlockSpec((LANE, 2 * RW_W), const),
            pl.BlockSpec((2 * LANE, RW_W), const),
            pl.BlockSpec((RW_W, LANE), const),
            pl.BlockSpec((LANE, RW_W), const),
        ],
        out_specs=(pl.BlockSpec((tm, RW_W), row),) * 8,
        scratch_shapes=[pltpu.VMEM((1, PR_W), F32)],
        compiler_params=_cparams(("arbitrary",)),
        name="rwkv_prep",
    )(pr, prev, mu_row, vecs, w_wa, w_g, ind, ind_t)


def _wkv_chunk_kernel(r_ref, lw_ref, k_ref, v_ref, kk_ref, b_ref, s0_ref, y_ref, s_ref):
    L = WKV_L

    @pl.when(pl.program_id(0) == 0)
    def _():
        s_ref[...] = s0_ref[...]

    lw = lw_ref[...]
    tril = (lax.broadcasted_iota(jnp.int32, (L, L), 0) >= lax.broadcasted_iota(jnp.int32, (L, L), 1)).astype(BF16)
    lw_hi, lw_lo = _split(lw)
    gcum = _dot(tril, lw_hi) + _dot(tril, lw_lo)
    e_g = jnp.exp(gcum)
    e_gi = jnp.exp(-gcum)
    rt = r_ref[...] * e_g
    kt = k_ref[...] * e_gi
    bt = b_ref[...] * e_gi
    kkt = kk_ref[...] * jnp.exp(gcum - lw)
    v = v_ref[...]

    row = lax.broadcasted_iota(jnp.int32, (2 * L, 2 * L), 0)
    col = lax.broadcasted_iota(jnp.int32, (2 * L, 2 * L), 1)
    same = (row // L) == (col // L)
    strict = same & ((col % L) < (row % L))
    incl = same & ((col % L) <= (row % L))
    head0 = lax.broadcasted_iota(jnp.int32, (L, 2 * L), 1) < RW_N

    def bd(x):
        return jnp.concatenate([jnp.where(head0, x, 0.0), jnp.where(head0, 0.0, x)], axis=0).astype(BF16)

    def dup(x):
        xb = x.astype(BF16)
        return jnp.concatenate([xb, xb], axis=0)

    pairs = range(RW_H // 2)
    sls = [slice(p * 2 * RW_N, (p + 1) * 2 * RW_N) for p in pairs]
    kkr_bd = [jnp.concatenate([bd(kkt[:, sl]), bd(rt[:, sl])], axis=0) for sl in sls]
    bk_bd = [jnp.concatenate([bd(bt[:, sl]), bd(kt[:, sl])], axis=0) for sl in sls]
    bk_dup = [jnp.concatenate([dup(bt[:, sl]), dup(kt[:, sl])], axis=0) for sl in sls]
    v_bd = [bd(v[:, sl]) for sl in sls]
    sp = [s_ref[p] for p in pairs]
    a_all = [_dot_nt(kkr_bd[p], bk_dup[p]) for p in pairs]
    from_state = [_dot_nt(kkr_bd[p], sp[p].astype(BF16)) for p in pairs]
    n = [jnp.where(strict, a[:2 * L, :2 * L], 0.0) for a in a_all]
    a_bk = [jnp.where(strict, a[:2 * L, 2 * L:], 0.0).astype(BF16) for a in a_all]
    a_r = [jnp.concatenate([jnp.where(incl, a[2 * L:, :2 * L], 0.0), jnp.where(incl, a[2 * L:, 2 * L:], 0.0)],
                           axis=1).astype(BF16) for a in a_all]
    x = [-(from_state[p][:2 * L] + _dot(a_bk[p], v_bd[p])) for p in pairs]
    x = [x[p] - _mm(n[p], x[p]) for p in pairs]
    steps = 1
    while 2 * steps < L:
        n = [_mm(n[p], n[p]) for p in pairs]
        x = [x[p] + _mm(n[p], x[p]) for p in pairs]
        steps *= 2
    dv = [jnp.concatenate([x[p].astype(BF16), v_bd[p]], axis=0) for p in pairs]
    for p in pairs:
        y_bd = from_state[p][2 * L:] + _dot(a_r[p], dv[p])
        y_ref[:, sls[p]] = y_bd[:L] + y_bd[L:]
        s_ref[p] = (sp[p] + _dot_tn(dv[p], bk_bd[p])) * e_g[L - 1:L, sls[p]]


def _mm(a, b):
    return _dot(a.astype(BF16), b.astype(BF16))


def _wkv_decode_kernel(pt_ref, r_ref, lw_ref, k_ref, v_ref, kk_ref, b_ref, s0_ref, q_ref, kn_ref, vn_ref, fn_ref, *refs,
                       group, steps_per_seq, n_seq, n_chunks):
    del pt_ref
    kc_refs, vc_refs, fc_refs = refs[:group], refs[group:2 * group], refs[2 * group:3 * group]
    u2_ref, gain_ref, y_ref, s_ref, oa_ref, m_ref, l_ref, acc_ref, carry_ref = refs[3 * group:]
    c = pl.program_id(0)
    _wkv_chunk_kernel(r_ref, lw_ref, k_ref, v_ref, kk_ref, b_ref, s0_ref, y_ref, s_ref)

    def decode():
        step = c % steps_per_seq
        _decode_pages(step == 0, step == steps_per_seq - 1, q_ref, kn_ref, vn_ref, fn_ref, kc_refs, vc_refs, fc_refs,
                      u2_ref, gain_ref, oa_ref, m_ref, l_ref, acc_ref, carry_ref)

    if n_seq * steps_per_seq == n_chunks:
        decode()
    else:
        pl.when(c < n_seq * steps_per_seq)(decode)


def _wkv_decode(r, lw, k, v, kk, b, s0_pairs, page_table, q, k_new, v_new, f_new, cache_k, cache_v, cache_ft, gain, layer):
    t = r.shape[0]
    n_chunks = t // WKV_L
    n_seq = q.shape[0]
    n_pages = page_table.shape[1]
    group = min(g for g in range(1, n_pages + 1) if n_pages % g == 0 and n_seq * (n_pages // g) <= n_chunks)
    assert group <= MAX_PAGES_PER_STEP
    steps_per_seq = n_pages // group
    attn_steps = n_seq * steps_per_seq
    last = n_pages - 1
    n_flat = PAGE_SIZE * FOX_H
    n_pairs = RW_H // 2
    u2 = (jnp.arange(PAGE_SIZE)[:, None] > (jnp.arange(n_flat) // FOX_H)[None, :]).astype(BF16)

    def seq_of(c):
        return jnp.minimum(c, attn_steps - 1) // steps_per_seq

    def page_of(c, pt, g):
        return pt[seq_of(c), last - ((jnp.minimum(c, attn_steps - 1) % steps_per_seq) * group + g)]

    rows = pl.BlockSpec((WKV_L, RW_W), lambda c, pt: (c, 0))
    sspec = pl.BlockSpec((n_pairs, LANE, LANE), lambda c, pt: (0, 0, 0))
    tok = pl.BlockSpec((1, FOX_H, FOX_HD), lambda c, pt: (seq_of(c), 0, 0))

    def page_kv(g):
        return pl.BlockSpec((None, 1, PAGE_SIZE, FOX_H, FOX_HD), lambda c, pt: (layer, page_of(c, pt, g), 0, 0, 0))

    def page_f(g):
        return pl.BlockSpec((None, 1, FOX_H, PAGE_SIZE), lambda c, pt: (layer, page_of(c, pt, g), 0, 0))

    stat = pltpu.VMEM((FOX_H, 1), F32)
    grid_spec = pltpu.PrefetchScalarGridSpec(
        num_scalar_prefetch=1,
        grid=(n_chunks,),
        in_specs=[
            rows, rows, rows, rows, rows, rows, sspec,
            tok, tok, tok,
            pl.BlockSpec((1, FOX_H, 1), lambda c, pt: (seq_of(c), 0, 0)),
            *[page_kv(g) for g in range(group)], *[page_kv(g) for g in range(group)],
            *[page_f(g) for g in range(group)],
            pl.BlockSpec((PAGE_SIZE, n_flat), lambda c, pt: (0, 0)),
            pl.BlockSpec((FOX_H, FOX_HD), lambda c, pt: (0, 0)),
        ],
        out_specs=(rows, sspec, tok),
        scratch_shapes=[stat, stat, pltpu.VMEM((FOX_H, FOX_HD), F32), stat],
    )
    return pl.pallas_call(
        functools.partial(_wkv_decode_kernel, group=group, steps_per_seq=steps_per_seq, n_seq=n_seq, n_chunks=n_chunks),
        out_shape=(jax.ShapeDtypeStruct((t, RW_W), F32), jax.ShapeDtypeStruct((n_pairs, LANE, LANE), F32),
                   jax.ShapeDtypeStruct((n_seq, FOX_H, FOX_HD), F32)),
        grid_spec=grid_spec,
        compiler_params=_cparams(("arbitrary",)),
        name="wkv_decode",
    )(page_table, r, lw, k, v, kk, b, s0_pairs, q, k_new, v_new, f_new, *([cache_k] * group), *([cache_v] * group),
      *([cache_ft] * group), u2, gain)


def _wkv_step_kernel(s_ref, r_ref, lw_ref, k_ref, v_ref, kk_ref, b_ref, y_ref, so_ref):
    s = s_ref[...]
    s_kk = jnp.sum(s * kk_ref[...], axis=-1, keepdims=True)
    s = s * jnp.exp(lw_ref[...]) - s_kk * b_ref[...] + v_ref[...] * k_ref[...]
    so_ref[...] = s
    y_ref[...] = jnp.sum(s * r_ref[...], axis=-1, keepdims=True)


def _wkv_step(state, r, lw, k, v, kk, b):
    bsz = state.shape[0]
    nb = 4 if bsz % 4 == 0 else 1
    keyed = lambda x: x.reshape(bsz, RW_H, 1, RW_N)
    kspec = pl.BlockSpec((nb, RW_H, 1, RW_N), lambda i: (i, 0, 0, 0))
    vspec = pl.BlockSpec((nb, RW_H, RW_N, 1), lambda i: (i, 0, 0, 0))
    sspec = pl.BlockSpec((nb, RW_H, RW_N, RW_N), lambda i: (i, 0, 0, 0))
    y, s_new = pl.pallas_call(
        _wkv_step_kernel,
        out_shape=(jax.ShapeDtypeStruct((bsz, RW_H, RW_N, 1), F32), jax.ShapeDtypeStruct(state.shape, F32)),
        grid=(bsz // nb,),
        in_specs=[sspec, kspec, kspec, kspec, vspec, kspec, kspec],
        out_specs=(vspec, sspec),
        compiler_params=_cparams(("parallel",)),
        name="wkv_step",
    )(state, keyed(r), keyed(lw), keyed(k), v.reshape(bsz, RW_H, RW_N, 1), keyed(kk), keyed(b))
    return y.reshape(bsz, RW_W), s_new


def _mix_out_kernel(x_ref, of_ref, y_ref, bonus_ref, g_ref, lnw_ref, lnb_ref, ind_ref, indt_ref, wo_ref, o_ref):
    y = y_ref[...]
    ind, ind_t = ind_ref[...], indt_ref[...]
    mean = _head_sum(y, ind, ind_t) * (1.0 / RW_N)
    yc = y - mean
    var = _head_sum(yc * yc, ind, ind_t) * (1.0 / RW_N)
    yn = yc * lax.rsqrt(var + LNX_EPS) * lnw_ref[...] + lnb_ref[...]
    o_rw = ((yn + bonus_ref[...]) * g_ref[...]).astype(BF16)
    o_ref[...] = (x_ref[...] + _dot(of_ref[...].astype(BF16), wo_ref[:FOX_W, :]) + _dot(o_rw, wo_ref[FOX_W:, :]))


def _mix_out(x, o_fox, y, bonus, g, lnw_row, lnb_row, ind, ind_t, w_out, tm):
    t, d = x.shape
    row = lambda i: (i, 0)
    const = lambda i: (0, 0)
    half = pl.BlockSpec((tm, RW_W), row)
    return pl.pallas_call(
        _mix_out_kernel,
        out_shape=jax.ShapeDtypeStruct((t, d), F32),
        grid=(t // tm,),
        in_specs=[
            pl.BlockSpec((tm, d), row), pl.BlockSpec((tm, FOX_W), row), half, half, half,
            pl.BlockSpec((1, RW_W), const), pl.BlockSpec((1, RW_W), const),
            pl.BlockSpec((RW_W, LANE), const), pl.BlockSpec((LANE, RW_W), const),
            pl.BlockSpec((d, d), const),
        ],
        out_specs=pl.BlockSpec((tm, d), row),
        compiler_params=_cparams(("parallel",)),
        name="mix_out",
    )(x, o_fox, y, bonus, g, lnw_row, lnb_row, ind, ind_t, w_out)


def _final_norm_kernel(x_ref, g_ref, o_ref):
    o_ref[...] = _rms(x_ref[...], g_ref[...])


def _final_norm(x, g, tm):
    t, d = x.shape
    return pl.pallas_call(
        _final_norm_kernel,
        out_shape=jax.ShapeDtypeStruct((t, d), F32),
        grid=(t // tm,),
        in_specs=[pl.BlockSpec((tm, d), lambda i: (i, 0)), pl.BlockSpec((1, d), lambda i: (0, 0))],
        out_specs=pl.BlockSpec((tm, d), lambda i: (i, 0)),
        compiler_params=_cparams(("parallel",)),
        name="final_norm",
    )(x, g)


def _pack_cols(a):
    lead = a.shape[:-1]
    z = lambda n: jnp.zeros(lead + (n,), a.dtype)
    return jnp.concatenate([a[..., :PR_WA + LANE], z(LANE), a[..., PR_WA + LANE:], z(2 * LANE - R_G)], axis=-1)


def _unpack_cols(a):
    return jnp.concatenate([a[..., :PR_WA + LANE], a[..., PR_G:PR_G + R_G]], axis=-1)


def _layer_weights(l, norm_ffa, ffa_w1, ffa_w3, ffa_w2, norm_mix, w_in, b_f, fox_gain, mu_shift, w0, w_up,
                   a0, a_up, g_up, k_k, k_a, r_k, lnx_w, lnx_b, w_out, norm_ffb, ffb_w1, ffb_w3, ffb_w2):
    d = D_MODEL
    w = w_in[l]
    pad = lambda n: jnp.zeros((d, n), F32)
    w_all = jnp.concatenate([
        w[:, FOX_W:3 * FOX_W], w[:, :FOX_W],
        w[:, C_FOX_IN:C_FOX_IN + PR_F], w[:, 3 * FOX_W:C_FOX_IN], pad(LANE - FOX_H),
        w[:, C_FOX_IN + PR_F:], pad(2 * LANE - R_G)], axis=1).astype(BF16)
    zeros = jnp.zeros((R_DECAY, RW_W), F32)
    w_wa = jnp.concatenate([jnp.concatenate([w_up[l], zeros], axis=1), jnp.concatenate([zeros, a_up[l]], axis=1)], axis=0)
    w_g = jnp.concatenate([g_up[l], jnp.zeros((2 * LANE - R_G, RW_W), F32)], axis=0)
    vecs = jnp.stack([w0[l], a0[l], k_k[l], k_a[l], r_k[l].reshape(RW_W)] + [jnp.zeros((RW_W,), F32)] * 3)
    return dict(
        norm_ffa=norm_ffa[l].reshape(1, d), ffa_w1=ffa_w1[l], ffa_w3=ffa_w3[l], ffa_w2=ffa_w2[l],
        norm_mix=norm_mix[l].reshape(1, d), w_all=w_all,
        bf_row=jnp.pad(b_f[l], (0, LANE - FOX_H)).reshape(1, LANE),
        gain_row=fox_gain[l].reshape(1, FOX_W), mu_row=_pack_cols(mu_shift[l]).reshape(1, PR_W),
        vecs=vecs, w_wa=w_wa.astype(BF16), w_g=w_g.astype(BF16),
        lnw_row=lnx_w[l].reshape(1, RW_W), lnb_row=lnx_b[l].reshape(1, RW_W), w_out=w_out[l].astype(BF16),
        norm_ffb=norm_ffb[l].reshape(1, d), ffb_w1=ffb_w1[l], ffb_w3=ffb_w3[l], ffb_w2=ffb_w2[l],
    )


def _head_indicators():
    lane_head = jnp.arange(RW_W) // RW_N
    ind = (lane_head[:, None] == jnp.arange(LANE)[None, :]).astype(BF16)
    return ind, ind.T


def _pairs_from_state(s):
    s = s.reshape(RW_H // 2, 2, RW_N, RW_N)
    z = jnp.zeros_like(s[:, 0])
    return jnp.concatenate([jnp.concatenate([s[:, 0], z], axis=2), jnp.concatenate([z, s[:, 1]], axis=2)], axis=1)


def _state_from_pairs(sp):
    return jnp.stack([sp[:, :RW_N, :RW_N], sp[:, RW_N:, RW_N:]], axis=1).reshape(RW_H, RW_N, RW_N)


def _tile(t, pref):
    return pref if t % pref == 0 else t


def _layer(xp, xs, lw_, ind, ind_t, layer, cache_k, cache_v, cache_ft, page_table, state, shift_prev):
    t, bsz = xp.shape[0], xs.shape[0]
    tm = _tile(t, 512)

    xs, *ffa_b = _ffn_cast(xs, lw_["norm_ffa"], lw_["ffa_w1"], lw_["ffa_w3"], lw_["ffa_w2"], 512)
    k_s, v_s, qkvb_s = _proj_qkv(xs, lw_["norm_mix"], lw_["w_all"], bsz)
    pr_s = _norm_matmul(xs, lw_["norm_mix"], lw_["w_all"], bsz, PR_TN, 3 * FOX_W, PR_W)
    logf_s = _logf_only(pr_s, lw_["bf_row"])
    q_s = qkvb_s[:, 2 * FOX_W:].reshape(bsz, FOX_H, FOX_HD)

    xp = _ffn(xp, lw_["norm_ffa"], *ffa_b, tm, 512)
    k_p, v_p, qkvb = _proj_qkv(xp, lw_["norm_mix"], lw_["w_all"], tm)
    pr = _norm_matmul(xp, lw_["norm_mix"], lw_["w_all"], tm, PR_TN, 3 * FOX_W, PR_W)
    logf, c, ct = _logf(pr, lw_["bf_row"], _tile(t, 256))
    o_fox = _fox_prompt(qkvb, c, ct, lw_["gain_row"], tm)
    r, lw, k, v, kk, b, g, bonus = _rwkv_prep(pr, lw_["mu_row"], lw_["mu_row"], lw_["vecs"], lw_["w_wa"], lw_["w_g"],
                                              ind, ind_t, _tile(t, 256), True)
    y, s_pairs, o_fox_s = _wkv_decode(r, lw, k, v, kk, b, jnp.zeros((RW_H // 2, LANE, LANE), F32), page_table, q_s, k_s,
                                      v_s, logf_s.reshape(bsz, FOX_H, 1), cache_k, cache_v, cache_ft,
                                      lw_["gain_row"].reshape(FOX_H, FOX_HD), layer)

    r_s, lw_s, kr_s, vr_s, kk_s, b_s, g_s, bonus_s = _rwkv_prep(
        pr_s, _pack_cols(shift_prev), lw_["mu_row"], lw_["vecs"], lw_["w_wa"], lw_["w_g"], ind, ind_t, bsz, False)
    y_s, s_new = _wkv_step(state, r_s, lw_s, kr_s, vr_s, kk_s, b_s)
    xs = _mix_out(xs, o_fox_s.reshape(bsz, FOX_W), y_s, bonus_s, g_s, lw_["lnw_row"], lw_["lnb_row"], ind, ind_t,
                  lw_["w_out"], bsz)
    xs, *ffb_b = _ffn_cast(xs, lw_["norm_ffb"], lw_["ffb_w1"], lw_["ffb_w3"], lw_["ffb_w2"], 512)
    out_s = (k_s.reshape(bsz, 1, FOX_H, FOX_HD), v_s.reshape(bsz, 1, FOX_H, FOX_HD), logf_s.reshape(bsz, 1, FOX_H),
             s_new, _unpack_cols(pr_s))

    xp = _mix_out(xp, o_fox, y, bonus, g, lw_["lnw_row"], lw_["lnb_row"], ind, ind_t, lw_["w_out"], _tile(t, 256))
    xp = _ffn(xp, lw_["norm_ffb"], *ffb_b, tm, 512)
    out_p = (k_p[None], v_p[None], logf.reshape(1, t, FOX_H), _state_from_pairs(s_pairs)[None],
             _unpack_cols(pr[t - 1:t, :]))
    return xp, xs, out_p, out_s


def kernel(x_prompt, x_sample, cache_k, cache_v, cache_logf, state_rwkv, state_shift, page_table, norm_ffa, ffa_w1, ffa_w3, ffa_w2, norm_mix, w_in, b_f, fox_gain, mu_shift, w0, w_up, a0, a_up, g_up, k_k, k_a, r_k, lnx_w, lnx_b, w_out, norm_ffb, ffb_w1, ffb_w3, ffb_w2, norm_final):
    depth = norm_ffa.shape[0]
    b_p, seq, d = x_prompt.shape
    b_s = x_sample.shape[0]
    assert b_p == 1 and x_sample.shape[1] == 1 and seq % WKV_L == 0
    cache_ft = jnp.swapaxes(cache_logf, 2, 3)
    ind, ind_t = _head_indicators()
    xp = x_prompt.reshape(seq, d)
    xs = x_sample.reshape(b_s, d)
    outs_p, outs_s = [], []
    for l in range(depth):
        lw_ = _layer_weights(l, norm_ffa, ffa_w1, ffa_w3, ffa_w2, norm_mix, w_in, b_f, fox_gain, mu_shift, w0, w_up,
                             a0, a_up, g_up, k_k, k_a, r_k, lnx_w, lnx_b, w_out, norm_ffb, ffb_w1, ffb_w3, ffb_w2)
        xp, xs, out_p, out_s = _layer(xp, xs, lw_, ind, ind_t, l, cache_k, cache_v, cache_ft, page_table,
                                      state_rwkv[l], state_shift[l])
        outs_p.append(out_p)
        outs_s.append(out_s)
    g_final = norm_final.reshape(1, d)
    y_prompt = _final_norm(xp, g_final, _tile(seq, 512)).reshape(b_p, seq, d)
    y_sample = _final_norm(xs, g_final, b_s).reshape(b_s, 1, d)
    stack = lambda outs, i: jnp.stack([o[i] for o in outs])
    return (y_prompt, y_sample,
            stack(outs_p, 0), stack(outs_p, 1), stack(outs_p, 2), stack(outs_p, 3), stack(outs_p, 4),
            stack(outs_s, 0), stack(outs_s, 1), stack(outs_s, 2), stack(outs_s, 3), stack(outs_s, 4))
```

```python
import functools

import jax
import jax.numpy as jnp
from jax import lax
from jax.experimental import pallas as pl
from jax.experimental.pallas import tpu as pltpu

F32 = jnp.float32
BF16 = jnp.bfloat16

D_MODEL = 2048
FOX_HD = 128
FOX_W = D_MODEL // 2
FOX_H = FOX_W // FOX_HD
RW_N = 64
RW_W = D_MODEL - FOX_W
RW_H = RW_W // RW_N
R_DECAY = 64
R_A = 64
R_G = 160
PAGE_SIZE = 128
C_FOX_IN = 3 * FOX_W + FOX_H
C_SHIFT = 3 * RW_W + R_DECAY + R_A + R_G
RMS_EPS = 1e-6
LNX_EPS = 64e-5
FFN_RES = 0.5

LOG2E = 1.4426950408889634
Q_SCALE = FOX_HD ** -0.5 * LOG2E
LANE = 128
PR_F = LANE - FOX_H
PR_R = LANE
PR_WA = PR_R + 3 * RW_W
PR_G = PR_WA + LANE
PR_W = PR_G + 2 * LANE
PR_TN = PR_W // 4
WKV_L = 64
HEAD_UNROLL = 4
MAX_PAGES_PER_STEP = 16
VMEM_LIMIT = 56 * 1024 * 1024


def _cparams(sem):
    return pltpu.CompilerParams(dimension_semantics=sem, vmem_limit_bytes=VMEM_LIMIT)


def _dot(a, b):
    return jnp.dot(a, b, preferred_element_type=F32)


def _dot_nt(a, b):
    return lax.dot_general(a, b, (((1,), (1,)), ((), ())), preferred_element_type=F32)


def _dot_tn(a, b):
    return lax.dot_general(a, b, (((0,), (0,)), ((), ())), preferred_element_type=F32)


def _split(x):
    hi = x.astype(BF16)
    lo = (x - hi.astype(F32)).astype(BF16)
    return hi, lo


def _rms(x, g):
    return x * lax.rsqrt(jnp.mean(x * x, axis=-1, keepdims=True) + RMS_EPS) * g


def _ffn_kernel(x_ref, g_ref, w1_ref, w3_ref, w2_ref, o_ref, hn_ref):
    @pl.when(pl.program_id(1) == 0)
    def _():
        x = x_ref[...]
        hn_ref[...] = _rms(x, g_ref[...]).astype(BF16)
        o_ref[...] = x

    hn = hn_ref[...]
    h1 = _dot(hn, w1_ref[...])
    h3 = _dot(hn, w3_ref[...])
    a = (h1 * jax.nn.sigmoid(h1) * h3 * FFN_RES).astype(BF16)
    o_ref[...] += _dot(a, w2_ref[...])


def _ffn(x, g, w1, w3, w2, tm, tf):
    t, d = x.shape
    f = w1.shape[1]
    return pl.pallas_call(
        _ffn_kernel,
        out_shape=jax.ShapeDtypeStruct((t, d), F32),
        grid=(t // tm, f // tf),
        in_specs=[
            pl.BlockSpec((tm, d), lambda i, j: (i, 0)),
            pl.BlockSpec((1, d), lambda i, j: (0, 0)),
            pl.BlockSpec((d, tf), lambda i, j: (0, j)),
            pl.BlockSpec((d, tf), lambda i, j: (0, j)),
            pl.BlockSpec((tf, d), lambda i, j: (j, 0)),
        ],
        out_specs=pl.BlockSpec((tm, d), lambda i, j: (i, 0)),
        scratch_shapes=[pltpu.VMEM((tm, d), BF16)],
        compiler_params=_cparams(("parallel", "arbitrary")),
        name="ffn",
    )(x, g, w1, w3, w2)


def _ffn_cast_kernel(x_ref, g_ref, w1_ref, w3_ref, w2_ref, o_ref, w1b_ref, w3b_ref, w2b_ref, hn_ref):
    w1b_ref[...] = w1_ref[...].astype(BF16)
    w3b_ref[...] = w3_ref[...].astype(BF16)
    w2b_ref[...] = w2_ref[...].astype(BF16)
    _ffn_kernel(x_ref, g_ref, w1b_ref, w3b_ref, w2b_ref, o_ref, hn_ref)


def _ffn_cast(x, g, w1, w3, w2, tf):
    t, d = x.shape
    f = w1.shape[1]
    up = pl.BlockSpec((d, tf), lambda i, j: (0, j))
    down = pl.BlockSpec((tf, d), lambda i, j: (j, 0))
    return pl.pallas_call(
        _ffn_cast_kernel,
        out_shape=(jax.ShapeDtypeStruct((t, d), F32), jax.ShapeDtypeStruct((d, f), BF16),
                   jax.ShapeDtypeStruct((d, f), BF16), jax.ShapeDtypeStruct((f, d), BF16)),
        grid=(1, f // tf),
        in_specs=[pl.BlockSpec((t, d), lambda i, j: (0, 0)), pl.BlockSpec((1, d), lambda i, j: (0, 0)), up, up, down],
        out_specs=(pl.BlockSpec((t, d), lambda i, j: (0, 0)), up, up, down),
        scratch_shapes=[pltpu.VMEM((t, d), BF16)],
        compiler_params=_cparams(("arbitrary", "arbitrary")),
        name="ffn_cast",
    )(x, g, w1, w3, w2)


def _norm_matmul_kernel(x_ref, g_ref, w_ref, o_ref, hn_ref):
    @pl.when(pl.program_id(1) == 0)
    def _():
        hn_ref[...] = _rms(x_ref[...], g_ref[...]).astype(BF16)

    o_ref[...] = _dot(hn_ref[...], w_ref[...])


def _norm_matmul(x, g, w, tm, tn, col0, n):
    t, d = x.shape
    first = col0 // tn
    assert col0 % tn == 0 and n % tn == 0
    return pl.pallas_call(
        _norm_matmul_kernel,
        out_shape=jax.ShapeDtypeStruct((t, n), F32),
        grid=(t // tm, n // tn),
        in_specs=[
            pl.BlockSpec((tm, d), lambda i, j: (i, 0)),
            pl.BlockSpec((1, d), lambda i, j: (0, 0)),
            pl.BlockSpec((d, tn), lambda i, j: (0, first + j)),
        ],
        out_specs=pl.BlockSpec((tm, tn), lambda i, j: (i, j)),
        scratch_shapes=[pltpu.VMEM((tm, d), BF16)],
        compiler_params=_cparams(("parallel", "arbitrary")),
        name="norm_matmul",
    )(x, g, w)


def _proj_qkv_kernel(x_ref, g_ref, w_ref, k_ref, v_ref, qkvb_ref, hn_ref):
    j = pl.program_id(1)

    @pl.when(j == 0)
    def _():
        hn_ref[...] = _rms(x_ref[...], g_ref[...]).astype(BF16)

    res = _dot(hn_ref[...], w_ref[...])

    def emit(head_ref):
        for h in range(FOX_H):
            head_ref[:, h, :] = res[:, h * FOX_HD:(h + 1) * FOX_HD]
        qkvb_ref[...] = res.astype(BF16)

    pl.when(j == 0)(lambda: emit(k_ref))
    pl.when(j == 1)(lambda: emit(v_ref))

    @pl.when(j == 2)
    def _():
        qkvb_ref[...] = (res * Q_SCALE).astype(BF16)


def _proj_qkv(x, g, w_kvq, tm):
    t, d = x.shape
    heads = jax.ShapeDtypeStruct((t, FOX_H, FOX_HD), F32)
    hspec = pl.BlockSpec((tm, FOX_H, FOX_HD), lambda i, j: (i, 0, 0))
    return pl.pallas_call(
        _proj_qkv_kernel,
        out_shape=(heads, heads, jax.ShapeDtypeStruct((t, 3 * FOX_W), BF16)),
        grid=(t // tm, 3),
        in_specs=[
            pl.BlockSpec((tm, d), lambda i, j: (i, 0)),
            pl.BlockSpec((1, d), lambda i, j: (0, 0)),
            pl.BlockSpec((d, FOX_W), lambda i, j: (0, (j + 1) % 3)),
        ],
        out_specs=(hspec, hspec, pl.BlockSpec((tm, FOX_W), lambda i, j: (i, j))),
        scratch_shapes=[pltpu.VMEM((tm, d), BF16)],
        compiler_params=_cparams(("parallel", "arbitrary")),
        name="proj_qkv",
    )(x, g, w_kvq)


def _log_sigmoid(z):
    return jnp.minimum(z, 0.0) - jnp.log1p(jnp.exp(-jnp.abs(z)))


def _logf_kernel(f_ref, bf_ref, tri_ref, lf_ref, c_ref, ct_ref, carry_ref):
    @pl.when(pl.program_id(0) == 0)
    def _():
        carry_ref[...] = jnp.zeros_like(carry_ref)

    lf = _log_sigmoid(f_ref[...] + bf_ref[...])
    hi, lo = _split(lf)
    c = _dot(tri_ref[...], hi) + _dot(tri_ref[...], lo) + carry_ref[...]
    carry_ref[...] = c[-1:, :]
    lf_ref[...] = lf[:, PR_F:]
    c2 = c * LOG2E
    c_ref[...] = c2[:, PR_F:]
    ct_ref[...] = c2.T[PR_F:, :]


def _logf(pr, bf_row, tm):
    t = pr.shape[0]
    tri = (lax.broadcasted_iota(jnp.int32, (tm, tm), 0) >= lax.broadcasted_iota(jnp.int32, (tm, tm), 1)).astype(BF16)
    return pl.pallas_call(
        _logf_kernel,
        out_shape=(
            jax.ShapeDtypeStruct((t, FOX_H), F32),
            jax.ShapeDtypeStruct((t, FOX_H), F32),
            jax.ShapeDtypeStruct((FOX_H, t), F32),
        ),
        grid=(t // tm,),
        in_specs=[
            pl.BlockSpec((tm, LANE), lambda i: (i, 0)),
            pl.BlockSpec((1, LANE), lambda i: (0, 0)),
            pl.BlockSpec((tm, tm), lambda i: (0, 0)),
        ],
        out_specs=(
            pl.BlockSpec((tm, FOX_H), lambda i: (i, 0)),
            pl.BlockSpec((tm, FOX_H), lambda i: (i, 0)),
            pl.BlockSpec((FOX_H, tm), lambda i: (0, i)),
        ),
        scratch_shapes=[pltpu.VMEM((1, LANE), F32)],
        compiler_params=_cparams(("arbitrary",)),
        name="logf_cumsum",
    )(pr, bf_row, tri)


def _logf_only_kernel(f_ref, bf_ref, lf_ref):
    lf_ref[...] = _log_sigmoid(f_ref[...] + bf_ref[...])[:, PR_F:]


def _logf_only(pr, bf_row):
    t = pr.shape[0]
    return pl.pallas_call(
        _logf_only_kernel,
        out_shape=jax.ShapeDtypeStruct((t, FOX_H), F32),
        grid=(1,),
        in_specs=[
            pl.BlockSpec((t, LANE), lambda i: (0, 0)),
            pl.BlockSpec((1, LANE), lambda i: (0, 0)),
        ],
        out_specs=pl.BlockSpec((t, FOX_H), lambda i: (0, 0)),
        name="logf",
    )(pr, bf_row)


def _fox_prompt_kernel(q_ref, k_ref, v_ref, cq_ref, ck_ref, gain_ref, o_ref, m_ref, l_ref, acc_ref, cqc_ref, *, tq):
    qi = pl.program_id(0)
    ki = pl.program_id(1)

    @pl.when(ki == 0)
    def _():
        m_ref[...] = jnp.full_like(m_ref, -jnp.inf)
        l_ref[...] = jnp.zeros_like(l_ref)
        acc_ref[...] = jnp.zeros_like(acc_ref)
        cq = cq_ref[...]
        for h in range(FOX_H):
            cqc_ref[h] = jnp.broadcast_to(cq[:, h:h + 1], (tq, LANE))

    def step(diagonal):
        tk = k_ref.shape[0]
        if diagonal:
            keep = (lax.broadcasted_iota(jnp.int32, (tq, tq), 0) >= lax.broadcasted_iota(jnp.int32, (tq, tq), 1))
        ones = jnp.ones((tk, LANE), BF16)

        def head(h, carry):
            sl = pl.ds(pl.multiple_of(h * FOX_HD, FOX_HD), FOX_HD)
            s = _dot_nt(q_ref[:, sl], k_ref[:, sl]) - ck_ref[pl.ds(h, 1), :]
            if diagonal:
                s = jnp.where(keep, s, -jnp.inf)
            cq = cqc_ref[h]
            m_prev = m_ref[h]
            m_new = jnp.maximum(m_prev, jnp.max(s, axis=-1, keepdims=True) + cq)
            shift = m_new - cq
            p = jnp.exp2(s - jnp.concatenate([shift] * (tk // LANE), axis=1)).astype(BF16)
            alpha = jnp.exp2(m_prev - m_new)
            pv = _dot(p, jnp.concatenate([v_ref[:, sl], ones], axis=1))
            l_ref[h] = alpha * l_ref[h] + pv[:, FOX_HD:]
            acc_ref[:, sl] = alpha * acc_ref[:, sl] + pv[:, :FOX_HD]
            m_ref[h] = m_new
            return carry

        lax.fori_loop(0, FOX_H, head, 0, unroll=HEAD_UNROLL)

    @pl.when(ki < qi)
    def _():
        step(False)

    @pl.when(ki == qi)
    def _():
        step(True)
        for h in range(FOX_H):
            sl = slice(h * FOX_HD, (h + 1) * FOX_HD)
            o = acc_ref[:, sl] / l_ref[h]
            o_ref[:, sl] = _rms(o, gain_ref[:, sl])


def _fox_prompt(qkvb, c, ct, gain_row, tq):
    t = qkvb.shape[0]
    n = t // tq
    stat = pltpu.VMEM((FOX_H, tq, LANE), F32)
    return pl.pallas_call(
        functools.partial(_fox_prompt_kernel, tq=tq),
        out_shape=jax.ShapeDtypeStruct((t, FOX_W), F32),
        grid=(n, n),
        in_specs=[
            pl.BlockSpec((tq, FOX_W), lambda i, j: (i, 2)),
            pl.BlockSpec((tq, FOX_W), lambda i, j: (jnp.minimum(i, j), 0)),
            pl.BlockSpec((tq, FOX_W), lambda i, j: (jnp.minimum(i, j), 1)),
            pl.BlockSpec((tq, FOX_H), lambda i, j: (i, 0)),
            pl.BlockSpec((FOX_H, tq), lambda i, j: (0, jnp.minimum(i, j))),
            pl.BlockSpec((1, FOX_W), lambda i, j: (0, 0)),
        ],
        out_specs=pl.BlockSpec((tq, FOX_W), lambda i, j: (i, 0)),
        scratch_shapes=[stat, stat, pltpu.VMEM((tq, FOX_W), F32), stat],
        compiler_params=_cparams(("parallel", "arbitrary")),
        name="fox_prompt",
    )(qkvb, qkvb, qkvb, c, ct, gain_row)


def _decode_pages(first, last, q_ref, kn_ref, vn_ref, fn_ref, kc_refs, vc_refs, fc_refs, u2_ref, gain_ref, o_ref,
                  m_ref, l_ref, acc_ref, carry_ref):
    n_flat = PAGE_SIZE * FOX_H
    q = q_ref[0].astype(F32)

    @pl.when(first)
    def _():
        m_ref[...] = jnp.sum(q * kn_ref[0], axis=-1, keepdims=True)
        l_ref[...] = jnp.ones_like(l_ref)
        acc_ref[...] = vn_ref[0]
        carry_ref[...] = fn_ref[0]

    qb = q.astype(BF16)
    own = ((lax.broadcasted_iota(jnp.int32, (FOX_H, n_flat), 1) % FOX_H)
           == lax.broadcasted_iota(jnp.int32, (FOX_H, n_flat), 0))
    carry = carry_ref[...]
    f_pages = [fc_ref[0] for fc_ref in fc_refs]
    halves = []
    for f_page in f_pages:
        hi = f_page.astype(BF16).astype(F32)
        halves += [hi, f_page - hi]
    suffix = _dot(jnp.concatenate(halves, axis=0).astype(BF16), u2_ref[...])
    scores = []
    for i, (kc_ref, f_page) in enumerate(zip(kc_refs, f_pages)):
        kf = kc_ref[0].reshape(n_flat, FOX_HD).astype(BF16)
        bias = carry + suffix[2 * i * FOX_H:(2 * i + 1) * FOX_H] + suffix[(2 * i + 1) * FOX_H:(2 * i + 2) * FOX_H]
        scores.append(jnp.where(own, _dot_nt(qb, kf) + bias * LOG2E, -jnp.inf))
        carry = carry + jnp.sum(f_page, axis=-1, keepdims=True)
    carry_ref[...] = carry

    m_prev = m_ref[...]
    m_new = m_prev
    for s in scores:
        m_new = jnp.maximum(m_new, jnp.max(s, axis=-1, keepdims=True))
    alpha = jnp.exp2(m_prev - m_new)
    l_new = alpha * l_ref[...]
    acc = alpha * acc_ref[...]
    for s, vc_ref in zip(scores, vc_refs):
        p = jnp.exp2(s - m_new)
        l_new = l_new + jnp.sum(p, axis=-1, keepdims=True)
        acc = acc + _dot(p.astype(BF16), vc_ref[0].reshape(n_flat, FOX_HD).astype(BF16))
    l_ref[...] = l_new
    acc_ref[...] = acc
    m_ref[...] = m_new

    @pl.when(last)
    def _():
        o_ref[0] = _rms(acc / l_new, gain_ref[...])


def _head_sum(x, ind, ind_t):
    hi, lo = _split(x)
    s = _dot(hi, ind) + _dot(lo, ind)
    s_hi, s_lo = _split(s)
    return _dot(s_hi, ind_t) + _dot(s_lo, ind_t)


def _rwkv_prep_kernel(pr_ref, prev_ref, mu_ref, vec_ref, wwa_ref, wg_ref, ind_ref, indt_ref,
                      r_ref, lw_ref, k_ref, v_ref, kk_ref, b_ref, g_ref, bonus_ref, carry_ref, *, sequential):
    pr = pr_ref[...]
    if sequential:
        @pl.when(pl.program_id(0) == 0)
        def _():
            carry_ref[...] = jnp.zeros_like(carry_ref)

        rolled = pltpu.roll(pr, 1, 0)
        first = lax.broadcasted_iota(jnp.int32, pr.shape, 0) == 0
        prev = jnp.where(first, carry_ref[...], rolled)
        carry_ref[...] = pr[-1:, :]
    else:
        prev = prev_ref[...]
    xs = pr + (prev - pr) * mu_ref[...]
    r = xs[:, PR_R:PR_R + RW_W]
    kr = xs[:, PR_R + RW_W:PR_R + 2 * RW_W]
    vr = xs[:, PR_R + 2 * RW_W:PR_WA]
    wa = xs[:, PR_WA:PR_WA + LANE]
    gd = xs[:, PR_G:PR_W]
    w0, a0, k_k, k_a, r_k = (vec_ref[i:i + 1, :] for i in range(5))

    is_decay = lax.broadcasted_iota(jnp.int32, wa.shape, 1) < R_DECAY
    wa_act = jnp.where(is_decay, jnp.tanh(wa), wa).astype(BF16)
    lora = _dot(wa_act, wwa_ref[...])
    x_w = w0 + lora[:, :RW_W]
    w_log = -(jnp.maximum(-x_w, 0.0) + jnp.log1p(jnp.exp(-jnp.abs(x_w)))) - 0.5
    lw_ref[...] = -jnp.exp(w_log)
    a = jax.nn.sigmoid(a0 + lora[:, RW_W:])
    g_ref[...] = _dot(jax.nn.sigmoid(gd).astype(BF16), wg_ref[...])

    kk = kr * k_k
    ss = _head_sum(kk * kk, ind_ref[...], indt_ref[...])
    kk = kk / jnp.maximum(jnp.sqrt(ss), 1e-12)
    k2 = kr * (1.0 + (a - 1.0) * k_a)
    bonus_ref[...] = _head_sum(r * k2 * r_k, ind_ref[...], indt_ref[...]) * vr
    r_ref[...] = r
    k_ref[...] = k2
    v_ref[...] = vr
    kk_ref[...] = kk
    b_ref[...] = kk * a


def _rwkv_prep(pr, prev, mu_row, vecs, w_wa, w_g, ind, ind_t, tm, sequential):
    t = pr.shape[0]
    row = lambda i: (i, 0)
    const = lambda i: (0, 0)
    out = jax.ShapeDtypeStruct((t, RW_W), F32)
    return pl.pallas_call(
        functools.partial(_rwkv_prep_kernel, sequential=sequential),
        out_shape=(out,) * 8,
        grid=(t // tm,),
        in_specs=[
            pl.BlockSpec((tm, PR_W), row),
            pl.BlockSpec((prev.shape[0] if sequential else tm, PR_W), const if sequential else row),
            pl.BlockSpec((1, PR_W), const),
            pl.BlockSpec((8, RW_W), const),
            pl.BlockSpec((LANE, 2 * RW_W), const),
            pl.BlockSpec((2 * LANE, RW_W), const),
            pl.BlockSpec((RW_W, LANE), const),
            pl.BlockSpec((LANE, RW_W), const),
        ],
        out_specs=(pl.BlockSpec((tm, RW_W), row),) * 8,
        scratch_shapes=[pltpu.VMEM((1, PR_W), F32)],
        compiler_params=_cparams(("arbitrary",)),
        name="rwkv_prep",
    )(pr, prev, mu_row, vecs, w_wa, w_g, ind, ind_t)


def _wkv_chunk_kernel(r_ref, lw_ref, k_ref, v_ref, kk_ref, b_ref, s0_ref, y_ref, s_ref):
    L = WKV_L

    @pl.when(pl.program_id(0) == 0)
    def _():
        s_ref[...] = s0_ref[...]

    lw = lw_ref[...]
    tril = (lax.broadcasted_iota(jnp.int32, (L, L), 0) >= lax.broadcasted_iota(jnp.int32, (L, L), 1)).astype(BF16)
    lw_hi, lw_lo = _split(lw)
    gcum = _dot(tril, lw_hi) + _dot(tril, lw_lo)
    e_g = jnp.exp(gcum)
    e_gi = jnp.exp(-gcum)
    rt = r_ref[...] * e_g
    kt = k_ref[...] * e_gi
    bt = b_ref[...] * e_gi
    kkt = kk_ref[...] * jnp.exp(gcum - lw)
    v = v_ref[...]

    row = lax.broadcasted_iota(jnp.int32, (2 * L, 2 * L), 0)
    col = lax.broadcasted_iota(jnp.int32, (2 * L, 2 * L), 1)
    same = (row // L) == (col // L)
    strict = same & ((col % L) < (row % L))
    incl = same & ((col % L) <= (row % L))
    head0 = lax.broadcasted_iota(jnp.int32, (L, 2 * L), 1) < RW_N

    def bd(x):
        return jnp.concatenate([jnp.where(head0, x, 0.0), jnp.where(head0, 0.0, x)], axis=0).astype(BF16)

    def dup(x):
        xb = x.astype(BF16)
        return jnp.concatenate([xb, xb], axis=0)

    pairs = range(RW_H // 2)
    sls = [slice(p * 2 * RW_N, (p + 1) * 2 * RW_N) for p in pairs]
    kkr_bd = [jnp.concatenate([bd(kkt[:, sl]), bd(rt[:, sl])], axis=0) for sl in sls]
    bk_bd = [jnp.concatenate([bd(bt[:, sl]), bd(kt[:, sl])], axis=0) for sl in sls]
    bk_dup = [jnp.concatenate([dup(bt[:, sl]), dup(kt[:, sl])], axis=0) for sl in sls]
    v_bd = [bd(v[:, sl]) for sl in sls]
    sp = [s_ref[p] for p in pairs]
    a_all = [_dot_nt(kkr_bd[p], bk_dup[p]) for p in pairs]
    from_state = [_dot_nt(kkr_bd[p], sp[p].astype(BF16)) for p in pairs]
    n = [jnp.where(strict, a[:2 * L, :2 * L], 0.0) for a in a_all]
    a_bk = [jnp.where(strict, a[:2 * L, 2 * L:], 0.0).astype(BF16) for a in a_all]
    a_r = [jnp.concatenate([jnp.where(incl, a[2 * L:, :2 * L], 0.0), jnp.where(incl, a[2 * L:, 2 * L:], 0.0)],
                           axis=1).astype(BF16) for a in a_all]
    x = [-(from_state[p][:2 * L] + _dot(a_bk[p], v_bd[p])) for p in pairs]
    x = [x[p] - _mm(n[p], x[p]) for p in pairs]
    steps = 1
    while 2 * steps < L:
        n = [_mm(n[p], n[p]) for p in pairs]
        x = [x[p] + _mm(n[p], x[p]) for p in pairs]
        steps *= 2
    dv = [jnp.concatenate([x[p].astype(BF16), v_bd[p]], axis=0) for p in pairs]
    for p in pairs:
        y_bd = from_state[p][2 * L:] + _dot(a_r[p], dv[p])
        y_ref[:, sls[p]] = y_bd[:L] + y_bd[L:]
        s_ref[p] = (sp[p] + _dot_tn(dv[p], bk_bd[p])) * e_g[L - 1:L, sls[p]]


def _mm(a, b):
    return _dot(a.astype(BF16), b.astype(BF16))


def _wkv_decode_kernel(pt_ref, r_ref, lw_ref, k_ref, v_ref, kk_ref, b_ref, s0_ref, q_ref, kn_ref, vn_ref, fn_ref, *refs,
                       group, steps_per_seq, n_seq, n_chunks):
    del pt_ref
    kc_refs, vc_refs, fc_refs = refs[:group], refs[group:2 * group], refs[2 * group:3 * group]
    u2_ref, gain_ref, y_ref, s_ref, oa_ref, m_ref, l_ref, acc_ref, carry_ref = refs[3 * group:]
    c = pl.program_id(0)
    _wkv_chunk_kernel(r_ref, lw_ref, k_ref, v_ref, kk_ref, b_ref, s0_ref, y_ref, s_ref)

    def decode():
        step = c % steps_per_seq
        _decode_pages(step == 0, step == steps_per_seq - 1, q_ref, kn_ref, vn_ref, fn_ref, kc_refs, vc_refs, fc_refs,
                      u2_ref, gain_ref, oa_ref, m_ref, l_ref, acc_ref, carry_ref)

    if n_seq * steps_per_seq == n_chunks:
        decode()
    else:
        pl.when(c < n_seq * steps_per_seq)(decode)


def _wkv_decode(r, lw, k, v, kk, b, s0_pairs, page_table, q, k_new, v_new, f_new, cache_k, cache_v, cache_ft, gain, layer):
    t = r.shape[0]
    n_chunks = t // WKV_L
    n_seq = q.shape[0]
    n_pages = page_table.shape[1]
    group = min(g for g in range(1, n_pages + 1) if n_pages % g == 0 and n_seq * (n_pages // g) <= n_chunks)
    assert group <= MAX_PAGES_PER_STEP
    steps_per_seq = n_pages // group
    attn_steps = n_seq * steps_per_seq
    last = n_pages - 1
    n_flat = PAGE_SIZE * FOX_H
    n_pairs = RW_H // 2
    u2 = (jnp.arange(PAGE_SIZE)[:, None] > (jnp.arange(n_flat) // FOX_H)[None, :]).astype(BF16)

    def seq_of(c):
        return jnp.minimum(c, attn_steps - 1) // steps_per_seq

    def page_of(c, pt, g):
        return pt[seq_of(c), last - ((jnp.minimum(c, attn_steps - 1) % steps_per_seq) * group + g)]

    rows = pl.BlockSpec((WKV_L, RW_W), lambda c, pt: (c, 0))
    sspec = pl.BlockSpec((n_pairs, LANE, LANE), lambda c, pt: (0, 0, 0))
    tok = pl.BlockSpec((1, FOX_H, FOX_HD), lambda c, pt: (seq_of(c), 0, 0))

    def page_kv(g):
        return pl.BlockSpec((None, 1, PAGE_SIZE, FOX_H, FOX_HD), lambda c, pt: (layer, page_of(c, pt, g), 0, 0, 0))

    def page_f(g):
        return pl.BlockSpec((None, 1, FOX_H, PAGE_SIZE), lambda c, pt: (layer, page_of(c, pt, g), 0, 0))

    stat = pltpu.VMEM((FOX_H, 1), F32)
    grid_spec = pltpu.PrefetchScalarGridSpec(
        num_scalar_prefetch=1,
        grid=(n_chunks,),
        in_specs=[
            rows, rows, rows, rows, rows, rows, sspec,
            tok, tok, tok,
            pl.BlockSpec((1, FOX_H, 1), lambda c, pt: (seq_of(c), 0, 0)),
            *[page_kv(g) for g in range(group)], *[page_kv(g) for g in range(group)],
            *[page_f(g) for g in range(group)],
            pl.BlockSpec((PAGE_SIZE, n_flat), lambda c, pt: (0, 0)),
            pl.BlockSpec((FOX_H, FOX_HD), lambda c, pt: (0, 0)),
        ],
        out_specs=(rows, sspec, tok),
        scratch_shapes=[stat, stat, pltpu.VMEM((FOX_H, FOX_HD), F32), stat],
    )
    return pl.pallas_call(
        functools.partial(_wkv_decode_kernel, group=group, steps_per_seq=steps_per_seq, n_seq=n_seq, n_chunks=n_chunks),
        out_shape=(jax.ShapeDtypeStruct((t, RW_W), F32), jax.ShapeDtypeStruct((n_pairs, LANE, LANE), F32),
                   jax.ShapeDtypeStruct((n_seq, FOX_H, FOX_HD), F32)),
        grid_spec=grid_spec,
        compiler_params=_cparams(("arbitrary",)),
        name="wkv_decode",
    )(page_table, r, lw, k, v, kk, b, s0_pairs, q, k_new, v_new, f_new, *([cache_k] * group), *([cache_v] * group),
      *([cache_ft] * group), u2, gain)


def _wkv_step_kernel(s_ref, r_ref, lw_ref, k_ref, v_ref, kk_ref, b_ref, y_ref, so_ref):
    s = s_ref[...]
    s_kk = jnp.sum(s * kk_ref[...], axis=-1, keepdims=True)
    s = s * jnp.exp(lw_ref[...]) - s_kk * b_ref[...] + v_ref[...] * k_ref[...]
    so_ref[...] = s
    y_ref[...] = jnp.sum(s * r_ref[...], axis=-1, keepdims=True)


def _wkv_step(state, r, lw, k, v, kk, b):
    bsz = state.shape[0]
    nb = 4 if bsz % 4 == 0 else 1
    keyed = lambda x: x.reshape(bsz, RW_H, 1, RW_N)
    kspec = pl.BlockSpec((nb, RW_H, 1, RW_N), lambda i: (i, 0, 0, 0))
    vspec = pl.BlockSpec((nb, RW_H, RW_N, 1), lambda i: (i, 0, 0, 0))
    sspec = pl.BlockSpec((nb, RW_H, RW_N, RW_N), lambda i: (i, 0, 0, 0))
    y, s_new = pl.pallas_call(
        _wkv_step_kernel,
        out_shape=(jax.ShapeDtypeStruct((bsz, RW_H, RW_N, 1), F32), jax.ShapeDtypeStruct(state.shape, F32)),
        grid=(bsz // nb,),
        in_specs=[sspec, kspec, kspec, kspec, vspec, kspec, kspec],
        out_specs=(vspec, sspec),
        compiler_params=_cparams(("parallel",)),
        name="wkv_step",
    )(state, keyed(r), keyed(lw), keyed(k), v.reshape(bsz, RW_H, RW_N, 1), keyed(kk), keyed(b))
    return y.reshape(bsz, RW_W), s_new


def _mix_out_kernel(x_ref, of_ref, y_ref, bonus_ref, g_ref, lnw_ref, lnb_ref, ind_ref, indt_ref, wo_ref, o_ref):
    y = y_ref[...]
    ind, ind_t = ind_ref[...], indt_ref[...]
    mean = _head_sum(y, ind, ind_t) * (1.0 / RW_N)
    yc = y - mean
    var = _head_sum(yc * yc, ind, ind_t) * (1.0 / RW_N)
    yn = yc * lax.rsqrt(var + LNX_EPS) * lnw_ref[...] + lnb_ref[...]
    o_rw = ((yn + bonus_ref[...]) * g_ref[...]).astype(BF16)
    o_ref[...] = (x_ref[...] + _dot(of_ref[...].astype(BF16), wo_ref[:FOX_W, :]) + _dot(o_rw, wo_ref[FOX_W:, :]))


def _mix_out(x, o_fox, y, bonus, g, lnw_row, lnb_row, ind, ind_t, w_out, tm):
    t, d = x.shape
    row = lambda i: (i, 0)
    const = lambda i: (0, 0)
    half = pl.BlockSpec((tm, RW_W), row)
    return pl.pallas_call(
        _mix_out_kernel,
        out_shape=jax.ShapeDtypeStruct((t, d), F32),
        grid=(t // tm,),
        in_specs=[
            pl.BlockSpec((tm, d), row), pl.BlockSpec((tm, FOX_W), row), half, half, half,
            pl.BlockSpec((1, RW_W), const), pl.BlockSpec((1, RW_W), const),
            pl.BlockSpec((RW_W, LANE), const), pl.BlockSpec((LANE, RW_W), const),
            pl.BlockSpec((d, d), const),
        ],
        out_specs=pl.BlockSpec((tm, d), row),
        compiler_params=_cparams(("parallel",)),
        name="mix_out",
    )(x, o_fox, y, bonus, g, lnw_row, lnb_row, ind, ind_t, w_out)


def _final_norm_kernel(x_ref, g_ref, o_ref):
    o_ref[...] = _rms(x_ref[...], g_ref[...])


def _final_norm(x, g, tm):
    t, d = x.shape
    return pl.pallas_call(
        _final_norm_kernel,
        out_shape=jax.ShapeDtypeStruct((t, d), F32),
        grid=(t // tm,),
        in_specs=[pl.BlockSpec((tm, d), lambda i: (i, 0)), pl.BlockSpec((1, d), lambda i: (0, 0))],
        out_specs=pl.BlockSpec((tm, d), lambda i: (i, 0)),
        compiler_params=_cparams(("parallel",)),
        name="final_norm",
    )(x, g)


def _pack_cols(a):
    return jnp.pad(a, [(0, 0)] * (a.ndim - 1) + [(PR_R, PR_W - PR_R - C_SHIFT)])


def _unpack_cols(a):
    return a[..., PR_R:PR_R + C_SHIFT]


def _layer_weights(l, norm_ffa, ffa_w1, ffa_w3, ffa_w2, norm_mix, w_in, b_f, fox_gain, mu_shift, w0, w_up,
                   a0, a_up, g_up, k_k, k_a, r_k, lnx_w, lnx_b, w_out, norm_ffb, ffb_w1, ffb_w3, ffb_w2):
    d = D_MODEL
    w = w_in[l].astype(BF16)
    w_rw = jnp.pad(w[:, 3 * FOX_W:], ((0, 0), (PR_F, PR_W - PR_F - FOX_H - C_SHIFT)))
    zeros = jnp.zeros((R_DECAY, RW_W), F32)
    w_wa = jnp.concatenate([jnp.concatenate([w_up[l], zeros], axis=1), jnp.concatenate([zeros, a_up[l]], axis=1)], axis=0)
    w_g = jnp.concatenate([g_up[l], jnp.zeros((2 * LANE - R_G, RW_W), F32)], axis=0)
    vecs = jnp.stack([w0[l], a0[l], k_k[l], k_a[l], r_k[l].reshape(RW_W)] + [jnp.zeros((RW_W,), F32)] * 3)
    return dict(
        norm_ffa=norm_ffa[l].reshape(1, d), ffa_w1=ffa_w1[l], ffa_w3=ffa_w3[l], ffa_w2=ffa_w2[l],
        norm_mix=norm_mix[l].reshape(1, d), w_qkv=w, w_rw=w_rw,
        bf_row=jnp.pad(b_f[l], (PR_F, 0)).reshape(1, LANE),
        gain_row=fox_gain[l].reshape(1, FOX_W), mu_row=_pack_cols(mu_shift[l]).reshape(1, PR_W),
        vecs=vecs, w_wa=w_wa.astype(BF16), w_g=w_g.astype(BF16),
        lnw_row=lnx_w[l].reshape(1, RW_W), lnb_row=lnx_b[l].reshape(1, RW_W), w_out=w_out[l].astype(BF16),
        norm_ffb=norm_ffb[l].reshape(1, d), ffb_w1=ffb_w1[l], ffb_w3=ffb_w3[l], ffb_w2=ffb_w2[l],
    )


def _head_indicators():
    lane_head = jnp.arange(RW_W) // RW_N
    ind = (lane_head[:, None] == jnp.arange(LANE)[None, :]).astype(BF16)
    return ind, ind.T


def _pairs_from_state(s):
    s = s.reshape(RW_H // 2, 2, RW_N, RW_N)
    z = jnp.zeros_like(s[:, 0])
    return jnp.concatenate([jnp.concatenate([s[:, 0], z], axis=2), jnp.concatenate([z, s[:, 1]], axis=2)], axis=1)


def _state_from_pairs(sp):
    return jnp.stack([sp[:, :RW_N, :RW_N], sp[:, RW_N:, RW_N:]], axis=1).reshape(RW_H, RW_N, RW_N)


def _tile(t, pref):
    return pref if t % pref == 0 else t


def _layer(xp, xs, lw_, ind, ind_t, layer, cache_k, cache_v, cache_ft, page_table, state, shift_prev):
    t, bsz = xp.shape[0], xs.shape[0]
    tm = _tile(t, 512)

    xs, *ffa_b = _ffn_cast(xs, lw_["norm_ffa"], lw_["ffa_w1"], lw_["ffa_w3"], lw_["ffa_w2"], 512)
    k_s, v_s, qkvb_s = _proj_qkv(xs, lw_["norm_mix"], lw_["w_qkv"], bsz)
    pr_s = _norm_matmul(xs, lw_["norm_mix"], lw_["w_rw"], bsz, PR_TN, 0, PR_W)
    logf_s = _logf_only(pr_s, lw_["bf_row"])
    q_s = qkvb_s[:, 2 * FOX_W:].reshape(bsz, FOX_H, FOX_HD)

    xp = _ffn(xp, lw_["norm_ffa"], *ffa_b, tm, 512)
    k_p, v_p, qkvb = _proj_qkv(xp, lw_["norm_mix"], lw_["w_qkv"], tm)
    pr = _norm_matmul(xp, lw_["norm_mix"], lw_["w_rw"], tm, PR_TN, 0, PR_W)
    logf, c, ct = _logf(pr, lw_["bf_row"], _tile(t, 256))
    o_fox = _fox_prompt(qkvb, c, ct, lw_["gain_row"], tm)
    r, lw, k, v, kk, b, g, bonus = _rwkv_prep(pr, lw_["mu_row"], lw_["mu_row"], lw_["vecs"], lw_["w_wa"], lw_["w_g"],
                                              ind, ind_t, _tile(t, 256), True)
    y, s_pairs, o_fox_s = _wkv_decode(r, lw, k, v, kk, b, jnp.zeros((RW_H // 2, LANE, LANE), F32), page_table, q_s, k_s,
                                      v_s, logf_s.reshape(bsz, FOX_H, 1), cache_k, cache_v, cache_ft,
                                      lw_["gain_row"].reshape(FOX_H, FOX_HD), layer)

    r_s, lw_s, kr_s, vr_s, kk_s, b_s, g_s, bonus_s = _rwkv_prep(
        pr_s, _pack_cols(shift_prev), lw_["mu_row"], lw_["vecs"], lw_["w_wa"], lw_["w_g"], ind, ind_t, bsz, False)
    y_s, s_new = _wkv_step(state, r_s, lw_s, kr_s, vr_s, kk_s, b_s)
    xs = _mix_out(xs, o_fox_s.reshape(bsz, FOX_W), y_s, bonus_s, g_s, lw_["lnw_row"], lw_["lnb_row"], ind, ind_t,
                  lw_["w_out"], bsz)
    xs, *ffb_b = _ffn_cast(xs, lw_["norm_ffb"], lw_["ffb_w1"], lw_["ffb_w3"], lw_["ffb_w2"], 512)
    out_s = (k_s.reshape(bsz, 1, FOX_H, FOX_HD), v_s.reshape(bsz, 1, FOX_H, FOX_HD), logf_s.reshape(bsz, 1, FOX_H),
             s_new, _unpack_cols(pr_s))

    xp = _mix_out(xp, o_fox, y, bonus, g, lw_["lnw_row"], lw_["lnb_row"], ind, ind_t, lw_["w_out"], _tile(t, 256))
    xp = _ffn(xp, lw_["norm_ffb"], *ffb_b, tm, 512)
    out_p = (k_p[None], v_p[None], logf.reshape(1, t, FOX_H), _state_from_pairs(s_pairs)[None],
             _unpack_cols(pr[t - 1:t, :]))
    return xp, xs, out_p, out_s


def kernel(x_prompt, x_sample, cache_k, cache_v, cache_logf, state_rwkv, state_shift, page_table, norm_ffa, ffa_w1, ffa_w3, ffa_w2, norm_mix, w_in, b_f, fox_gain, mu_shift, w0, w_up, a0, a_up, g_up, k_k, k_a, r_k, lnx_w, lnx_b, w_out, norm_ffb, ffb_w1, ffb_w3, ffb_w2, norm_final):
    depth = norm_ffa.shape[0]
    b_p, seq, d = x_prompt.shape
    b_s = x_sample.shape[0]
    assert b_p == 1 and x_sample.shape[1] == 1 and seq % WKV_L == 0
    cache_ft = jnp.swapaxes(cache_logf, 2, 3)
    ind, ind_t = _head_indicators()
    xp = x_prompt.reshape(seq, d)
    xs = x_sample.reshape(b_s, d)
    outs_p, outs_s = [], []
    for l in range(depth):
        lw_ = _layer_weights(l, norm_ffa, ffa_w1, ffa_w3, ffa_w2, norm_mix, w_in, b_f, fox_gain, mu_shift, w0, w_up,
                             a0, a_up, g_up, k_k, k_a, r_k, lnx_w, lnx_b, w_out, norm_ffb, ffb_w1, ffb_w3, ffb_w2)
        xp, xs, out_p, out_s = _layer(xp, xs, lw_, ind, ind_t, l, cache_k, cache_v, cache_ft, page_table,
                                      state_rwkv[l], state_shift[l])
        outs_p.append(out_p)
        outs_s.append(out_s)
    g_final = norm_final.reshape(1, d)
    y_prompt = _final_norm(xp, g_final, _tile(seq, 512)).reshape(b_p, seq, d)
    y_sample = _final_norm(xs, g_final, b_s).reshape(b_s, 1, d)
    stack = lambda outs, i: jnp.stack([o[i] for o in outs])
    return (y_prompt, y_sample,
            stack(outs_p, 0), stack(outs_p, 1), stack(outs_p, 2), stack(outs_p, 3), stack(outs_p, 4),
            stack(outs_s, 0), stack(outs_s, 1), stack(outs_s, 2), stack(outs_s, 3), stack(outs_s, 4))
```
